```python
import math, functools
import jax, jax.numpy as jnp
from jax import lax
import numpy as np

D_MODEL = 1024
BATCH = 2
SEQ = 8192
DEPTH = 2
DEC_BATCH = 128
DEC_SEQ = 1
PAST_LEN = 2048
PAGE_SIZE = 128

MIX_WIDTH = D_MODEL
POOL_WIDTH = MIX_WIDTH // 2
POOL_WINDOWS = (2, 4, 8, 16)
N_POOL_GROUPS = len(POOL_WINDOWS)
POOL_GROUP = POOL_WIDTH // N_POOL_GROUPS
POOL_PAST = max(POOL_WINDOWS) - 1
ATTN_WIDTH = MIX_WIDTH - POOL_WIDTH
DIFF_HD = 64
N_DIFF_HEADS = ATTN_WIDTH // (2 * DIFF_HD)
QK_WIDTH = N_DIFF_HEADS * 2 * DIFF_HD
IN_WIDTH = POOL_WIDTH + 2 * QK_WIDTH + ATTN_WIDTH
N_MEM = 256
N_CROSS_HEADS = 4
CROSS_HD = D_MODEL // N_CROSS_HEADS
D_FF = 2816
CONV_W = 3
Q_BLOCK = 128
EPS = 1e-6
SUBLN_EPS = 1e-5

kernel_name = 'hymba_pool_diffattn_convffn_step'


def rmsnorm(x, g, eps=EPS):
    xf = x.astype(jnp.float32)
    y = xf * lax.rsqrt(jnp.mean(xf * xf, axis=-1, keepdims=True) + eps)
    return (y * g.astype(jnp.float32)).astype(x.dtype)


def alibi_slopes():
    return jnp.asarray(2.0 ** (-8.0 * np.arange(1, N_DIFF_HEADS + 1) / N_DIFF_HEADS), jnp.float32)


def lambda_init(layer_idx):
    return 0.8 - 0.6 * math.exp(-0.3 * layer_idx)


def pool_mix(u, prefix, start_pos, w_pool, scale):
    B, T, C = u.shape
    up = jnp.concatenate([prefix.astype(u.dtype), u], axis=1)
    upf = up.astype(jnp.float32)
    cs = jnp.concatenate([jnp.zeros((B, 1, C), jnp.float32), jnp.cumsum(upf, axis=1)], axis=1)
    end = cs[:, POOL_PAST + 1:]
    pos = start_pos + jnp.arange(T)
    uf = u.astype(jnp.float32)
    groups = []
    for g, w in enumerate(POOL_WINDOWS):
        sl = slice(g * POOL_GROUP, (g + 1) * POOL_GROUP)
        wsum = end[..., sl] - cs[:, POOL_PAST + 1 - w:POOL_PAST + 1 - w + T, sl]
        cnt = jnp.minimum(pos + 1, w).astype(jnp.float32)[None, :, None]
        groups.append(wsum / cnt - uf[..., sl])
    p = jnp.stack(groups, axis=2)
    y = jnp.einsum('btgc,gcd->btgd', p, w_pool.astype(jnp.float32)).reshape(B, T, C)
    y = y * scale.astype(jnp.float32)
    return y.astype(u.dtype), up[:, -POOL_PAST:]


def diff_attn_block(q, k, v, q_pos, k_pos, lam):
    s = jnp.einsum('bqhcd,bkhcd->bhcqk', q.astype(jnp.float32), k.astype(jnp.float32)) * (DIFF_HD ** -0.5)
    dist = (q_pos[:, None] - k_pos[None, :])
    bias = -alibi_slopes()[:, None, None] * dist.astype(jnp.float32)[None]
    s = jnp.where((dist >= 0)[None, None, None], s + bias[None, :, None], -jnp.inf)
    a = jax.nn.softmax(s, axis=-1)
    w = a[:, :, 0] - lam * a[:, :, 1]
    o = jnp.einsum('bhqk,bkhe->bqhe', w, v.astype(jnp.float32))
    return o.astype(v.dtype)


def diff_attn_prompt(q, k, v, lam):
    B, T = q.shape[0], q.shape[1]
    nb = T // Q_BLOCK
    qb = jnp.moveaxis(q.reshape(B, nb, Q_BLOCK, N_DIFF_HEADS, 2, DIFF_HD), 1, 0)
    k_pos = jnp.arange(T)

    def one_block(args):
        qi, i = args
        return diff_attn_block(qi, k, v, i * Q_BLOCK + jnp.arange(Q_BLOCK), k_pos, lam)

    o = lax.map(one_block, (qb, jnp.arange(nb)))
    return jnp.moveaxis(o, 0, 1).reshape(B, T, N_DIFF_HEADS, 2 * DIFF_HD)


def diff_attn_sample(q, k, v, lam, past_k, past_v):
    B, T = q.shape[0], q.shape[1]
    past = past_k.shape[1]
    kk = jnp.concatenate([past_k.astype(k.dtype), k.reshape(B, T, N_DIFF_HEADS, 2 * DIFF_HD)], axis=1)
    kk = kk.reshape(B, past + T, N_DIFF_HEADS, 2, DIFF_HD)
    vv = jnp.concatenate([past_v.astype(v.dtype), v], axis=1)
    return diff_attn_block(q, kk, vv, past + jnp.arange(T), jnp.arange(past + T), lam)


def cross_attn(h, mem_k, mem_v, wq, wo):
    B, T, _ = h.shape
    q = (h @ wq).reshape(B, T, N_CROSS_HEADS, CROSS_HD)
    s = jnp.einsum('bqhd,bkhd->bhqk', q.astype(jnp.float32), mem_k.astype(jnp.float32)) * (CROSS_HD ** -0.5)
    a = jax.nn.softmax(s, axis=-1)
    o = jnp.einsum('bhqk,bkhd->bqhd', a, mem_v.astype(jnp.float32)).reshape(B, T, D_MODEL)
    return o.astype(h.dtype) @ wo


def conv_ffn(h, prefix, w_up, conv_w, conv_b, w_down):
    T = h.shape[1]
    up = h @ w_up
    upp = jnp.concatenate([prefix.astype(up.dtype), up], axis=1)
    c = conv_b
    for j in range(CONV_W):
        c = c + upp[:, j:j + T] * conv_w[j]
    a, g = c[..., :D_FF], c[..., D_FF:]
    return (jax.nn.silu(g) * a) @ w_down, upp[:, -(CONV_W - 1):]


def trunk_layer(x, layer_idx, attend, start_pos, pool_prefix, conv_prefix, mem_k, mem_v,
                g_mix, w_in, pool_w, pool_scale, lam_q1, lam_k1, lam_q2, lam_k2, subln_g, w_out,
                g_cross, wq_c, wo_c, g_ffn, w_up, conv_w, conv_b, w_down):
    B, T, _ = x.shape
    h = rmsnorm(x, g_mix)
    z = h @ w_in
    u = z[..., :POOL_WIDTH]
    q = z[..., POOL_WIDTH:POOL_WIDTH + QK_WIDTH].reshape(B, T, N_DIFF_HEADS, 2, DIFF_HD)
    k = z[..., POOL_WIDTH + QK_WIDTH:POOL_WIDTH + 2 * QK_WIDTH].reshape(B, T, N_DIFF_HEADS, 2, DIFF_HD)
    v = z[..., POOL_WIDTH + 2 * QK_WIDTH:].reshape(B, T, N_DIFF_HEADS, 2 * DIFF_HD)
    pool_out, pool_state = pool_mix(u, pool_prefix, start_pos, pool_w, pool_scale)
    lam_0 = lambda_init(layer_idx)
    lam = (jnp.exp(jnp.sum(lam_q1.astype(jnp.float32) * lam_k1.astype(jnp.float32)))
           - jnp.exp(jnp.sum(lam_q2.astype(jnp.float32) * lam_k2.astype(jnp.float32))) + lam_0)
    o = attend(q, k, v, lam)
    o = rmsnorm(o, subln_g, SUBLN_EPS) * (1.0 - lam_0)
    mix = jnp.concatenate([pool_out, o.reshape(B, T, ATTN_WIDTH).astype(pool_out.dtype)], axis=-1)
    x = x + mix @ w_out
    x = x + cross_attn(rmsnorm(x, g_cross), mem_k, mem_v, wq_c, wo_c)
    f, conv_state = conv_ffn(rmsnorm(x, g_ffn), conv_prefix, w_up, conv_w, conv_b, w_down)
    x = x + f
    return x, k.reshape(B, T, N_DIFF_HEADS, 2 * DIFF_HD), v, pool_state, conv_state


def setup_inputs(seed: int = 0) -> dict:
    key = jax.random.key(seed)
    ks = jax.random.split(key, 40)
    n_pages = PAST_LEN // PAGE_SIZE
    n_used = DEC_BATCH * n_pages
    n_phys = n_used + n_used // 4

    def nrm(k, shape, s):
        return jax.random.normal(k, shape, jnp.float32) * s

    def gain(k, shape):
        return 1.0 + 0.1 * jax.random.normal(k, shape, jnp.float32)

    page_table = jax.random.permutation(ks[9], n_phys)[:n_used].reshape(DEC_BATCH, n_pages).astype(jnp.int32)
    return {
        'x_prompt': nrm(ks[0], (BATCH, SEQ, D_MODEL), 1.0),
        'x_sample': nrm(ks[1], (DEC_BATCH, DEC_SEQ, D_MODEL), 1.0),
        'mem_prompt': nrm(ks[2], (BATCH, N_MEM, D_MODEL), 1.0),
        'cache_k': nrm(ks[3], (DEPTH, n_phys, PAGE_SIZE, N_DIFF_HEADS, 2 * DIFF_HD), 1.0),
        'cache_v': nrm(ks[4], (DEPTH, n_phys, PAGE_SIZE, N_DIFF_HEADS, 2 * DIFF_HD), 1.0),
        'cache_mem_k': nrm(ks[5], (DEPTH, DEC_BATCH, N_MEM, N_CROSS_HEADS, CROSS_HD), 1.0),
        'cache_mem_v': nrm(ks[6], (DEPTH, DEC_BATCH, N_MEM, N_CROSS_HEADS, CROSS_HD), 1.0),
        'state_pool': nrm(ks[7], (DEPTH, DEC_BATCH, POOL_PAST, POOL_WIDTH), 1.0),
        'state_conv': nrm(ks[8], (DEPTH, DEC_BATCH, CONV_W - 1, 2 * D_FF), 1.0),
        'page_table': page_table,
        'g_mix': gain(ks[10], (DEPTH, D_MODEL)),
        'w_in': nrm(ks[11], (DEPTH, D_MODEL, IN_WIDTH), D_MODEL ** -0.5),
        'pool_w': nrm(ks[12], (DEPTH, N_POOL_GROUPS, POOL_GROUP, POOL_GROUP), POOL_GROUP ** -0.5),
        'pool_scale': gain(ks[13], (DEPTH, POOL_WIDTH)),
        'lam_q1': nrm(ks[14], (DEPTH, DIFF_HD), 0.1),
        'lam_k1': nrm(ks[15], (DEPTH, DIFF_HD), 0.1),
        'lam_q2': nrm(ks[16], (DEPTH, DIFF_HD), 0.1),
        'lam_k2': nrm(ks[17], (DEPTH, DIFF_HD), 0.1),
        'subln_g': gain(ks[18], (DEPTH, 2 * DIFF_HD)),
        'w_out': nrm(ks[19], (DEPTH, MIX_WIDTH, D_MODEL), MIX_WIDTH ** -0.5),
        'g_cross': gain(ks[20], (DEPTH, D_MODEL)),
        'wq_c': nrm(ks[21], (DEPTH, D_MODEL, D_MODEL), D_MODEL ** -0.5),
        'wk_c': nrm(ks[22], (DEPTH, D_MODEL, D_MODEL), D_MODEL ** -0.5),
        'wv_c': nrm(ks[23], (DEPTH, D_MODEL, D_MODEL), D_MODEL ** -0.5),
        'wo_c': nrm(ks[24], (DEPTH, D_MODEL, D_MODEL), D_MODEL ** -0.5),
        'g_ffn': gain(ks[25], (DEPTH, D_MODEL)),
        'w_up': nrm(ks[26], (DEPTH, D_MODEL, 2 * D_FF), D_MODEL ** -0.5),
        'conv_w': nrm(ks[27], (DEPTH, CONV_W, 2 * D_FF), 0.5),
        'conv_b': nrm(ks[28], (DEPTH, 2 * D_FF), 0.01),
        'w_down': nrm(ks[29], (DEPTH, D_FF, D_MODEL), D_FF ** -0.5),
        'g_final': gain(ks[30], (D_MODEL,)),
    }


def reference(x_prompt, x_sample, mem_prompt, cache_k, cache_v, cache_mem_k, cache_mem_v,
              state_pool, state_conv, page_table,
              g_mix, w_in, pool_w, pool_scale, lam_q1, lam_k1, lam_q2, lam_k2, subln_g, w_out,
              g_cross, wq_c, wk_c, wv_c, wo_c, g_ffn, w_up, conv_w, conv_b, w_down, g_final):
    Bp = x_prompt.shape[0]
    Bs = x_sample.shape[0]
    xp, xs = x_prompt, x_sample
    kp_l, vp_l, mkp_l, mvp_l, pp_l, cp_l = [], [], [], [], [], []
    ks_l, vs_l, ps_l, cs_l = [], [], [], []
    for l in range(DEPTH):
        lw = (g_mix[l], w_in[l], pool_w[l], pool_scale[l], lam_q1[l], lam_k1[l], lam_q2[l], lam_k2[l],
              subln_g[l], w_out[l], g_cross[l], wq_c[l], wo_c[l], g_ffn[l], w_up[l], conv_w[l], conv_b[l], w_down[l])
        mem_k = (mem_prompt @ wk_c[l]).reshape(Bp, N_MEM, N_CROSS_HEADS, CROSS_HD)
        mem_v = (mem_prompt @ wv_c[l]).reshape(Bp, N_MEM, N_CROSS_HEADS, CROSS_HD)
        pool0 = jnp.zeros((Bp, POOL_PAST, POOL_WIDTH), xp.dtype)
        conv0 = jnp.zeros((Bp, CONV_W - 1, 2 * D_FF), xp.dtype)
        xp, kp, vp, pp, cp = trunk_layer(xp, l, diff_attn_prompt, 0, pool0, conv0, mem_k, mem_v, *lw)
        kp_l.append(kp); vp_l.append(vp); mkp_l.append(mem_k); mvp_l.append(mem_v)
        pp_l.append(pp); cp_l.append(cp)
        past_k = cache_k[l][page_table].reshape(Bs, -1, N_DIFF_HEADS, 2 * DIFF_HD)
        past_v = cache_v[l][page_table].reshape(Bs, -1, N_DIFF_HEADS, 2 * DIFF_HD)
        attend_s = functools.partial(diff_attn_sample, past_k=past_k, past_v=past_v)
        xs, kn, vn, ps, cs = trunk_layer(xs, l, attend_s, past_k.shape[1], state_pool[l], state_conv[l],
                                         cache_mem_k[l], cache_mem_v[l], *lw)
        ks_l.append(kn); vs_l.append(vn); ps_l.append(ps); cs_l.append(cs)
    y_prompt = rmsnorm(xp, g_final)
    y_sample = rmsnorm(xs, g_final)
    return (y_prompt, y_sample,
            jnp.stack(kp_l), jnp.stack(vp_l), jnp.stack(mkp_l), jnp.stack(mvp_l),
            jnp.stack(pp_l), jnp.stack(cp_l),
            jnp.stack(ks_l), jnp.stack(vs_l), jnp.stack(ps_l), jnp.stack(cs_l))
```

```python
import functools
import math

import jax
import jax.numpy as jnp
import numpy as np
from jax import lax
from jax.experimental import pallas as pl
from jax.experimental.pallas import tpu as pltpu

F32 = jnp.float32
BF16 = jnp.bfloat16

D_MODEL = 1024
POOL_WIDTH = 512
POOL_WINDOWS = (2, 4, 8, 16)
POOL_GROUP = 128
POOL_PAST = 15
N_HEADS = 4
HEAD_W = 128
DIFF_HD = 64
QK_WIDTH = 512
IN_WIDTH = 2048
N_MEM = 256
N_CROSS_HEADS = 4
CROSS_HD = 256
D_FF = 2816
CONV_W = 3
EPS = 1e-6
SUBLN_EPS = 1e-5
PAGE = 128

LANES = 128
SUBLANES = 8
VMEM_LIMIT = 56 * 1024 * 1024

ROW_TILE = 512
TQ = 256
TK = 256
FF_CHUNKS = ((0, 1024), (1024, 1024), (2048, 768))
NEG_INF = float("-inf")


def _lambda_init(layer_idx):
    return 0.8 - 0.6 * math.exp(-0.3 * layer_idx)


def _rms(x, g, eps):
    ms = jnp.mean(x * x, axis=-1, keepdims=True)
    return x * lax.rsqrt(ms + eps) * g


def _params(sem, vmem=VMEM_LIMIT):
    return pltpu.CompilerParams(dimension_semantics=sem, vmem_limit_bytes=vmem)


def _const_spec(shape):
    nd = len(shape)
    return pl.BlockSpec(shape, lambda *_: (0,) * nd)


def _pool_groups(u, shifted, cnt_fn, pw_ref, ps_ref, po_ref):
    for g, w in enumerate(POOL_WINDOWS):
        cols = slice(g * POOL_GROUP, (g + 1) * POOL_GROUP)
        ug = u[:, cols]
        ws = ug
        for j in range(1, w):
            ws = ws + shifted(j, cols)
        p = ws / cnt_fn(w) - ug
        y = jnp.dot(p.astype(BF16), pw_ref[g], preferred_element_type=F32)
        po_ref[:, cols] = (y * ps_ref[:, cols]).astype(po_ref.dtype)


def _in_proj_prompt_kernel(x_ref, g_ref, w_ref, pw_ref, ps_ref,
                           k_ref, v_ref, qt_ref, kb_ref, vt_ref, po_ref, tail_ref, ext_ref, *, tm):
    t = pl.program_id(1)
    halo = 2 * SUBLANES

    @pl.when(t == 0)
    def _():
        ext_ref[0:halo, :] = jnp.zeros((halo, POOL_WIDTH), F32)

    h = _rms(x_ref[...], g_ref[...], EPS).astype(BF16)
    z = jnp.dot(h, w_ref[...], preferred_element_type=F32)
    u = z[:, :POOL_WIDTH]
    k_ref[...] = z[:, POOL_WIDTH + QK_WIDTH:POOL_WIDTH + 2 * QK_WIDTH]
    v_ref[...] = z[:, POOL_WIDTH + 2 * QK_WIDTH:]
    scale = DIFF_HD ** -0.5
    for hh in range(N_HEADS):
        q0 = POOL_WIDTH + hh * HEAD_W
        k0 = POOL_WIDTH + QK_WIDTH + hh * HEAD_W
        v0 = POOL_WIDTH + 2 * QK_WIDTH + hh * HEAD_W
        for s in range(tm // TQ):
            rows = slice(s * TQ, (s + 1) * TQ)
            qt_ref[hh, s] = (z[rows, q0:q0 + HEAD_W] * scale).T.astype(BF16)
            vt_ref[hh, s] = z[rows, v0:v0 + HEAD_W].T.astype(BF16)
            kb_ref[hh, s] = z[rows, k0:k0 + HEAD_W].astype(BF16)

    ext_ref[halo:halo + tm, :] = u
    pos = t * tm + lax.broadcasted_iota(jnp.int32, (tm, POOL_GROUP), 0)

    def shifted(j, cols):
        return ext_ref[halo - j:halo - j + tm, cols]

    def cnt(w):
        return jnp.minimum(pos + 1, w).astype(F32)

    _pool_groups(u, shifted, cnt, pw_ref, ps_ref, po_ref)
    tail_ref[...] = u[tm - halo:, :]
    ext_ref[0:halo, :] = u[tm - halo:, :]


def _in_proj_prompt(x, g, w_in, pool_w, pool_scale):
    B, T, D = x.shape
    tm = ROW_TILE
    nt = T // tm
    spt = tm // TQ
    row = lambda b, t: (b, t, 0)
    out_shape = (
        jax.ShapeDtypeStruct((B, T, QK_WIDTH), F32),
        jax.ShapeDtypeStruct((B, T, QK_WIDTH), F32),
        jax.ShapeDtypeStruct((B, N_HEADS, T // TQ, HEAD_W, TQ), BF16),
        jax.ShapeDtypeStruct((B, N_HEADS, T // TK, TK, HEAD_W), BF16),
        jax.ShapeDtypeStruct((B, N_HEADS, T // TK, HEAD_W, TK), BF16),
        jax.ShapeDtypeStruct((B, T, POOL_WIDTH), BF16),
        jax.ShapeDtypeStruct((B, 2 * SUBLANES, POOL_WIDTH), F32),
    )
    tile5 = lambda b, t: (b, 0, t, 0, 0)
    return pl.pallas_call(
        functools.partial(_in_proj_prompt_kernel, tm=tm),
        grid=(B, nt),
        in_specs=[
            pl.BlockSpec((None, tm, D), row),
            _const_spec((1, D)),
            _const_spec((D, IN_WIDTH)),
            _const_spec((len(POOL_WINDOWS), POOL_GROUP, POOL_GROUP)),
            _const_spec((1, POOL_WIDTH)),
        ],
        out_specs=(
            pl.BlockSpec((None, tm, QK_WIDTH), row),
            pl.BlockSpec((None, tm, QK_WIDTH), row),
            pl.BlockSpec((None, N_HEADS, spt, HEAD_W, TQ), tile5),
            pl.BlockSpec((None, N_HEADS, spt, TK, HEAD_W), tile5),
            pl.BlockSpec((None, N_HEADS, spt, HEAD_W, TK), tile5),
            pl.BlockSpec((None, tm, POOL_WIDTH), row),
            pl.BlockSpec((None, 2 * SUBLANES, POOL_WIDTH), lambda b, t: (b, 0, 0)),
        ),
        out_shape=out_shape,
        scratch_shapes=[pltpu.VMEM((tm + 2 * SUBLANES, POOL_WIDTH), F32)],
        compiler_params=_params(("arbitrary", "arbitrary")),
        name="in_proj_prompt",
    )(x, g, w_in, pool_w, pool_scale)


def _lambda_value(lamp_ref, lam0):
    lp = lamp_ref[...]
    s1 = jnp.sum(lp[0:1] * lp[1:2], axis=1, keepdims=True)
    s2 = jnp.sum(lp[2:3] * lp[3:4], axis=1, keepdims=True)
    return jnp.exp(s1) - jnp.exp(s2) + lam0


def _flash_kernel(slopes_ref, lamp_ref, g_ref, qt_ref, k_ref, vt_ref, o_ref,
                  rhs_ref, m_ref, acc_ref, *, lam0):
    hh = pl.program_id(1)
    i = pl.program_id(2)
    slope = slopes_ref[hh]
    two_tq = 2 * TQ

    qt = qt_ref[...]
    z64 = jnp.zeros((DIFF_HD, TQ), BF16)
    top = jnp.concatenate([qt[:DIFF_HD], z64], axis=1)
    mid = jnp.concatenate([z64, qt[DIFF_HD:]], axis=1)
    row = lax.broadcasted_iota(jnp.int32, (HEAD_W, two_tq), 0)
    col = lax.broadcasted_iota(jnp.int32, (HEAD_W, two_tq), 1)
    r_f = (col & (TQ - 1)).astype(F32)
    bot = jnp.where(row == 0, -slope * r_f, jnp.where(row == 1, slope, 0.0)).astype(BF16)
    rhs_ref[...] = jnp.concatenate([top, mid, bot], axis=0)
    m_ref[...] = jnp.full((1, two_tq), NEG_INF, F32)
    acc_ref[...] = jnp.zeros(acc_ref.shape, F32)

    krow = lax.broadcasted_iota(jnp.int32, (TK, HEAD_W), 0)
    klane = lax.broadcasted_iota(jnp.int32, (TK, HEAD_W), 1)
    aug_k = jnp.where(klane == 0, 1.0, jnp.where(klane == 1, krow.astype(F32), 0.0)).astype(BF16)
    ones_rows = jnp.ones((2 * SUBLANES, TK), BF16)

    def step(j, masked):
        kk = jnp.concatenate([k_ref[j], aug_k], axis=1)
        s = jnp.dot(kk, rhs_ref[...], preferred_element_type=F32)
        if masked:
            c_idx = lax.broadcasted_iota(jnp.int32, (TK, two_tq), 0)
            r_idx = lax.broadcasted_iota(jnp.int32, (TK, two_tq), 1) & (TQ - 1)
            s = jnp.where(c_idx <= r_idx, s, NEG_INF)
        dlt = slope * ((i - j) * TQ).astype(F32)
        m_old = m_ref[...]
        m_new = jnp.maximum(m_old, jnp.max(s, axis=0, keepdims=True) - dlt)
        p = jnp.exp(s - (m_new + dlt))
        alpha = jnp.exp(m_old - m_new)
        v1 = jnp.concatenate([vt_ref[j], ones_rows], axis=0)
        pv = jnp.dot(v1, p.astype(BF16), preferred_element_type=F32)
        acc_ref[...] = alpha * acc_ref[...] + pv
        m_ref[...] = m_new

    def body(j, carry):
        step(j, False)
        return carry

    lax.fori_loop(0, i, body, 0)
    step(i, True)

    lam = _lambda_value(lamp_ref, lam0)
    acc = acc_ref[...]
    o1 = acc[:HEAD_W, :TQ] / acc[HEAD_W:HEAD_W + 1, :TQ]
    o2 = acc[:HEAD_W, TQ:] / acc[HEAD_W:HEAD_W + 1, TQ:]
    ot = o1 - lam * o2
    ms = jnp.mean(ot * ot, axis=0, keepdims=True)
    y = ot * lax.rsqrt(ms + SUBLN_EPS) * g_ref[...] * (1.0 - lam0)
    o_ref[...] = y.T.astype(o_ref.dtype)


def _flash_diff_attn(qt, kb, vt, lamp, subln_g, lam0):
    B, H, nq = qt.shape[0], qt.shape[1], qt.shape[2]
    nk = kb.shape[2]
    T = nq * TQ
    slopes = jnp.asarray(2.0 ** (-8.0 * np.arange(1, N_HEADS + 1) / N_HEADS), F32)
    return pl.pallas_call(
        functools.partial(_flash_kernel, lam0=lam0),
        grid=(B, H, nq),
        in_specs=[
            pl.BlockSpec(memory_space=pltpu.SMEM),
            _const_spec((4, DIFF_HD)),
            _const_spec((HEAD_W, 1)),
            pl.BlockSpec((None, None, None, HEAD_W, TQ), lambda b, h, i: (b, h, i, 0, 0)),
            pl.BlockSpec((None, None, nk, TK, HEAD_W), lambda b, h, i: (b, h, 0, 0, 0)),
            pl.BlockSpec((None, None, nk, HEAD_W, TK), lambda b, h, i: (b, h, 0, 0, 0)),
        ],
        out_specs=pl.BlockSpec((None, TQ, HEAD_W), lambda b, h, i: (b, i, h)),
        out_shape=jax.ShapeDtypeStruct((B, T, H * HEAD_W), BF16),
        scratch_shapes=[
            pltpu.VMEM((2 * HEAD_W, 2 * TQ), BF16),
            pltpu.VMEM((1, 2 * TQ), F32),
            pltpu.VMEM((HEAD_W + 2 * SUBLANES, 2 * TQ), F32),
        ],
        compiler_params=_params(("arbitrary", "arbitrary", "arbitrary")),
        name="flash_diff_attn",
    )(slopes, lamp, subln_g, qt, kb, vt)


def _out_q_kernel(x_ref, po_ref, o_ref, sg_ref, wout_ref, gc_ref, wq_ref, x1_ref, qc_ref, *, subln_scale):
    o = o_ref[...]
    if subln_scale is not None:
        parts = []
        for hh in range(N_HEADS):
            oh = o[:, hh * HEAD_W:(hh + 1) * HEAD_W]
            parts.append(_rms(oh, sg_ref[...], SUBLN_EPS) * subln_scale)
        o = jnp.concatenate(parts, axis=1)
    mix = jnp.dot(po_ref[...], wout_ref[0:POOL_WIDTH, :], preferred_element_type=F32)
    mix = mix + jnp.dot(o.astype(BF16), wout_ref[POOL_WIDTH:, :], preferred_element_type=F32)
    x1 = x_ref[...] + mix
    x1_ref[...] = x1
    hc = _rms(x1, gc_ref[...], EPS).astype(BF16)
    qc = jnp.dot(hc, wq_ref[...], preferred_element_type=F32) * (CROSS_HD ** -0.5)
    qc_ref[...] = qc.astype(BF16)


def _out_q(x, po, o, subln_g_row, w_out, g_cross, wq_c, tm, subln_scale=None):
    M, D = x.shape
    row = lambda t: (t, 0)
    return pl.pallas_call(
        functools.partial(_out_q_kernel, subln_scale=subln_scale),
        grid=(M // tm,),
        in_specs=[
            pl.BlockSpec((tm, D), row),
            pl.BlockSpec((tm, POOL_WIDTH), row),
            pl.BlockSpec((tm, N_HEADS * HEAD_W), row),
            _const_spec((1, HEAD_W)),
            _const_spec((D, D)),
            _const_spec((1, D)),
            _const_spec((D, D)),
        ],
        out_specs=(pl.BlockSpec((tm, D), row), pl.BlockSpec((tm, D), row)),
        out_shape=(jax.ShapeDtypeStruct((M, D), F32), jax.ShapeDtypeStruct((M, D), BF16)),
        compiler_params=_params(("arbitrary",)),
        name="out_q",
    )(x, po, o, subln_g_row, w_out, g_cross, wq_c)


def _mem_kv_kernel(mem_ref, wk_ref, wv_ref, mk_ref, mv_ref, mkt_ref, mvb_ref):
    m = mem_ref[...].astype(BF16)
    k = jnp.dot(m, wk_ref[...], preferred_element_type=F32)
    v = jnp.dot(m, wv_ref[...], preferred_element_type=F32)
    mk_ref[...] = k
    mv_ref[...] = v
    mkt_ref[...] = k.T.astype(BF16)
    mvb_ref[...] = v.astype(BF16)


def _mem_kv(mem, wk, wv):
    B, N, D = mem.shape
    blk = lambda b: (b, 0, 0)
    return pl.pallas_call(
        _mem_kv_kernel,
        grid=(B,),
        in_specs=[pl.BlockSpec((None, N, D), blk), _const_spec((D, D)), _const_spec((D, D))],
        out_specs=(pl.BlockSpec((None, N, D), blk), pl.BlockSpec((None, N, D), blk),
                   pl.BlockSpec((None, D, N), blk), pl.BlockSpec((None, N, D), blk)),
        out_shape=(jax.ShapeDtypeStruct((B, N, D), F32), jax.ShapeDtypeStruct((B, N, D), F32),
                   jax.ShapeDtypeStruct((B, D, N), BF16), jax.ShapeDtypeStruct((B, N, D), BF16)),
        compiler_params=_params(("arbitrary",)),
        name="mem_kv",
    )(mem, wk, wv)


def _cross_prompt_kernel(q_ref, kt_ref, v_ref, o_ref):
    for hh in range(N_CROSS_HEADS):
        cols = slice(hh * CROSS_HD, (hh + 1) * CROSS_HD)
        s = jnp.dot(q_ref[:, cols], kt_ref[cols, :], preferred_element_type=F32)
        m = jnp.max(s, axis=1, keepdims=True)
        p = jnp.exp(s - m)
        l = jnp.sum(p, axis=1, keepdims=True)
        o = jnp.dot(p.astype(BF16), v_ref[:, cols], preferred_element_type=F32) / l
        o_ref[:, cols] = o.astype(o_ref.dtype)


def _cross_prompt(qc, mkt, mvb, tm):
    B, T, D = qc.shape
    row = lambda b, t: (b, t, 0)
    blk = lambda b, t: (b, 0, 0)
    return pl.pallas_call(
        _cross_prompt_kernel,
        grid=(B, T // tm),
        in_specs=[pl.BlockSpec((None, tm, D), row),
                  pl.BlockSpec((None, D, N_MEM), blk),
                  pl.BlockSpec((None, N_MEM, D), blk)],
        out_specs=pl.BlockSpec((None, tm, D), row),
        out_shape=jax.ShapeDtypeStruct((B, T, D), BF16),
        compiler_params=_params(("arbitrary", "arbitrary")),
        name="cross_prompt",
    )(qc, mkt, mvb)


def _ffn_kernel(*refs, seq_mode, final_norm, tm):
    if seq_mode:
        (x1_ref, oc_ref, wo_ref, g_ref, wup_ref, cw_ref, cb_ref, wdn_ref, gfin_ref,
         out_ref, up_out_ref, acc_ref, h_ref, ext_ref, halo_ref) = refs
    else:
        (x1_ref, oc_ref, p2_ref, p1_ref, wo_ref, g_ref, wup_ref, cw_ref, cb_ref, wdn_ref, gfin_ref,
         out_ref, up_out_ref, acc_ref, h_ref) = refs
    t = pl.program_id(1)

    if seq_mode:
        @pl.when(t == 0)
        def _():
            halo_ref[...] = jnp.zeros(halo_ref.shape, F32)

    x2 = x1_ref[...] + jnp.dot(oc_ref[...], wo_ref[...], preferred_element_type=F32)
    acc_ref[...] = x2
    h_ref[...] = _rms(x2, g_ref[...], EPS).astype(BF16)

    def conv_part(col0, cw):
        cols = slice(col0, col0 + cw)
        up = jnp.dot(h_ref[...], wup_ref[:, cols], preferred_element_type=F32)
        if seq_mode:
            ext_ref[0:SUBLANES, 0:cw] = halo_ref[:, cols]
            ext_ref[SUBLANES:SUBLANES + tm, 0:cw] = up
            um1 = ext_ref[SUBLANES - 1:SUBLANES - 1 + tm, 0:cw]
            um2 = ext_ref[SUBLANES - 2:SUBLANES - 2 + tm, 0:cw]
            halo_ref[:, cols] = up[tm - SUBLANES:, :]
            up_out_ref[:, cols] = up[tm - SUBLANES:, :]
        else:
            um1 = p1_ref[:, cols]
            um2 = p2_ref[:, cols]
            up_out_ref[:, cols] = up
        c = cb_ref[:, cols] + um2 * cw_ref[0:1, cols]
        c = c + um1 * cw_ref[1:2, cols]
        return c + up * cw_ref[2:3, cols]

    for c0, cw in FF_CHUNKS:
        a = conv_part(c0, cw)
        g = conv_part(D_FF + c0, cw)
        act = (g * (1.0 / (1.0 + jnp.exp(-g))) * a).astype(BF16)
        acc_ref[...] += jnp.dot(act, wdn_ref[c0:c0 + cw, :], preferred_element_type=F32)

    x3 = acc_ref[...]
    if final_norm:
        x3 = _rms(x3, gfin_ref[...], EPS)
    out_ref[...] = x3


def _ffn(x1, oc, prev, wo, g_ffn, w_up, conv_w, conv_b, w_down, g_final, tm, final_norm):
    B, T, D = x1.shape
    seq_mode = prev is None
    W2 = 2 * D_FF
    row = lambda b, t: (b, t, 0)
    max_cw = max(cw for _, cw in FF_CHUNKS)
    in_specs = [pl.BlockSpec((None, tm, D), row), pl.BlockSpec((None, tm, D), row)]
    args = [x1, oc]
    if not seq_mode:
        in_specs += [pl.BlockSpec((None, tm, W2), row), pl.BlockSpec((None, tm, W2), row)]
        args += list(prev)
    in_specs += [_const_spec((D, D)), _const_spec((1, D)), _const_spec((D, W2)),
                 _const_spec((CONV_W, W2)), _const_spec((1, W2)), _const_spec((D_FF, D)),
                 _const_spec((1, D))]
    args += [wo, g_ffn, w_up, conv_w, conv_b, w_down, g_final]
    scratch = [pltpu.VMEM((tm, D), F32), pltpu.VMEM((tm, D), BF16)]
    if seq_mode:
        up_rows = SUBLANES
        up_spec = pl.BlockSpec((None, SUBLANES, W2), lambda b, t: (b, 0, 0))
        scratch += [pltpu.VMEM((tm + SUBLANES, max_cw), F32), pltpu.VMEM((SUBLANES, W2), F32)]
    else:
        up_rows = T
        up_spec = pl.BlockSpec((None, tm, W2), row)
    return pl.pallas_call(
        functools.partial(_ffn_kernel, seq_mode=seq_mode, final_norm=final_norm, tm=tm),
        grid=(B, T // tm),
        in_specs=in_specs,
        out_specs=(pl.BlockSpec((None, tm, D), row), up_spec),
        out_shape=(jax.ShapeDtypeStruct((B, T, D), F32), jax.ShapeDtypeStruct((B, up_rows, W2), F32)),
        scratch_shapes=scratch,
        compiler_params=_params(("arbitrary", "arbitrary")),
        name="wo_ffn_seq" if seq_mode else "wo_ffn_step",
    )(*args)


def _in_proj_sample_kernel(x_ref, g_ref, w_ref, st_ref, pw_ref, ps_ref, z_ref, po_ref, nst_ref):
    h = _rms(x_ref[...], g_ref[...], EPS).astype(BF16)
    z = jnp.dot(h, w_ref[...], preferred_element_type=F32)
    z_ref[...] = z
    u = z[:, :POOL_WIDTH]

    def shifted(j, cols):
        return st_ref[POOL_PAST - j, :, cols]

    _pool_groups(u, shifted, lambda w: float(w), pw_ref, ps_ref, po_ref)
    for r in range(POOL_PAST - 1):
        nst_ref[r] = st_ref[r + 1]
    nst_ref[POOL_PAST - 1] = u


def _in_proj_sample(x, g, w_in, st_t, pool_w, pool_scale):
    M, D = x.shape
    return pl.pallas_call(
        _in_proj_sample_kernel,
        grid=(1,),
        in_specs=[_const_spec((M, D)), _const_spec((1, D)), _const_spec((D, IN_WIDTH)),
                  _const_spec((POOL_PAST, M, POOL_WIDTH)),
                  _const_spec((len(POOL_WINDOWS), POOL_GROUP, POOL_GROUP)),
                  _const_spec((1, POOL_WIDTH))],
        out_specs=(_const_spec((M, IN_WIDTH)), _const_spec((M, POOL_WIDTH)),
                   _const_spec((POOL_PAST, M, POOL_WIDTH))),
        out_shape=(jax.ShapeDtypeStruct((M, IN_WIDTH), F32),
                   jax.ShapeDtypeStruct((M, POOL_WIDTH), BF16),
                   jax.ShapeDtypeStruct((POOL_PAST, M, POOL_WIDTH), F32)),
        compiler_params=_params(("arbitrary",)),
        name="in_proj_sample",
    )(x, g, w_in, st_t, pool_w, pool_scale)


def _paged_attn_kernel(pt_ref, lamp_ref, z_ref, bias_ref, expand_ref, *rest, n_pages, lam0):
    del pt_ref
    k_refs = rest[:n_pages]
    v_refs = rest[n_pages:2 * n_pages]
    o_ref = rest[2 * n_pages]
    s_ref = rest[2 * n_pages + 1]
    n_past = n_pages * PAGE
    n_lanes = 2 * N_HEADS

    z = z_ref[...]
    q = z[:, POOL_WIDTH:POOL_WIDTH + QK_WIDTH] * (DIFF_HD ** -0.5)
    k_new = z[:, POOL_WIDTH + QK_WIDTH:POOL_WIDTH + 2 * QK_WIDTH]
    v_new = z[:, POOL_WIDTH + 2 * QK_WIDTH:]
    rj = lax.broadcasted_iota(jnp.int32, (LANES, QK_WIDTH), 0)
    cj = lax.broadcasted_iota(jnp.int32, (LANES, QK_WIDTH), 1) >> 6
    q_blk = jnp.where(rj == cj, jnp.broadcast_to(q, (LANES, QK_WIDTH)), 0.0).astype(BF16)
    nt_dims = (((1,), (1,)), ((), ()))

    for p in range(n_pages):
        kp = k_refs[p][...].astype(BF16)
        sp = lax.dot_general(kp, q_blk, nt_dims, preferred_element_type=F32)
        s_ref[p * PAGE:(p + 1) * PAGE, :] = sp + bias_ref[p * PAGE:(p + 1) * PAGE, :]
    kn = jnp.broadcast_to(k_new, (SUBLANES, QK_WIDTH)).astype(BF16)
    s_new = lax.dot_general(kn, q_blk, nt_dims, preferred_element_type=F32)
    first = lax.broadcasted_iota(jnp.int32, (SUBLANES, LANES), 0) == 0
    s_ref[n_past:n_past + SUBLANES, :] = jnp.where(first, s_new, NEG_INF)

    s = s_ref[...]
    m = jnp.max(s, axis=0, keepdims=True)
    pr = jnp.exp(s - m)
    l = jnp.sum(pr, axis=0, keepdims=True)
    lam = _lambda_value(lamp_ref, lam0)
    lane = lax.broadcasted_iota(jnp.int32, (1, LANES), 1)
    coef = jnp.where(lane >= n_lanes, 0.0, jnp.where((lane & 1) == 0, 1.0, -lam)) / l
    s_ref[...] = pr * coef

    acc = jnp.zeros((SUBLANES, N_HEADS * HEAD_W), F32)
    for p in range(n_pages):
        wp = s_ref[p * PAGE:(p + 1) * PAGE, :].astype(BF16)
        wb = jnp.dot(wp, expand_ref[...], preferred_element_type=F32)
        prod = wb * v_refs[p][...]
        acc = acc + jnp.sum(prod.reshape(PAGE // SUBLANES, SUBLANES, N_HEADS * HEAD_W), axis=0)
    w_new = s_ref[n_past:n_past + SUBLANES, :].astype(BF16)
    wb_new = jnp.dot(w_new, expand_ref[...], preferred_element_type=F32)
    acc = acc + wb_new * v_new
    o_ref[...] = jnp.sum(acc, axis=0, keepdims=True)


def _paged_attn(z3, page_table, cache_k_l, cache_v_l, lamp, lam0):
    Bs = z3.shape[0]
    n_pages = page_table.shape[1]
    n_past = n_pages * PAGE
    n_phys = cache_k_l.shape[0]
    ck = cache_k_l.reshape(n_phys, PAGE, N_HEADS * HEAD_W)
    cv = cache_v_l.reshape(n_phys, PAGE, N_HEADS * HEAD_W)
    slopes = 2.0 ** (-8.0 * np.arange(1, N_HEADS + 1) / N_HEADS)
    lane_slope = np.zeros((LANES,), np.float32)
    lane_slope[:2 * N_HEADS] = np.repeat(slopes, 2)
    dist = (n_past - np.arange(n_past, dtype=np.float32))[:, None]
    bias = jnp.asarray(-dist * lane_slope[None, :], F32)
    expand = np.zeros((LANES, N_HEADS * HEAD_W), np.float32)
    for j in range(2 * N_HEADS):
        expand[j, (j // 2) * HEAD_W:(j // 2 + 1) * HEAD_W] = 1.0
    expand = jnp.asarray(expand, BF16)

    def page_spec(p):
        return pl.BlockSpec((None, PAGE, N_HEADS * HEAD_W), lambda b, pt: (pt[b, p], 0, 0))

    const2 = lambda b, pt: (0, 0)
    grid_spec = pltpu.PrefetchScalarGridSpec(
        num_scalar_prefetch=1,
        grid=(Bs,),
        in_specs=[pl.BlockSpec((4, DIFF_HD), const2),
                  pl.BlockSpec((None, 1, IN_WIDTH), lambda b, pt: (b, 0, 0)),
                  pl.BlockSpec((n_past, LANES), const2),
                  pl.BlockSpec((LANES, N_HEADS * HEAD_W), const2)]
                 + [page_spec(p) for p in range(n_pages)]
                 + [page_spec(p) for p in range(n_pages)],
        out_specs=pl.BlockSpec((None, 1, N_HEADS * HEAD_W), lambda b, pt: (b, 0, 0)),
        scratch_shapes=[pltpu.VMEM((n_past + SUBLANES, LANES), F32)],
    )
    return pl.pallas_call(
        functools.partial(_paged_attn_kernel, n_pages=n_pages, lam0=lam0),
        grid_spec=grid_spec,
        out_shape=jax.ShapeDtypeStruct((Bs, 1, N_HEADS * HEAD_W), F32),
        compiler_params=_params(("arbitrary",)),
        name="paged_diff_attn",
    )(page_table, lamp, z3, bias, expand, *([ck] * n_pages), *([cv] * n_pages))


def _cross_sample_kernel(q_ref, expand_ref, k_ref, v_ref, o_ref):
    q = q_ref[...].astype(F32)
    rj = lax.broadcasted_iota(jnp.int32, (LANES, D_MODEL), 0)
    cj = lax.broadcasted_iota(jnp.int32, (LANES, D_MODEL), 1) >> 8
    q_blk = jnp.where(rj == cj, jnp.broadcast_to(q, (LANES, D_MODEL)), 0.0).astype(BF16)
    s = lax.dot_general(k_ref[...].astype(BF16), q_blk, (((1,), (1,)), ((), ())),
                        preferred_element_type=F32)
    m = jnp.max(s, axis=0, keepdims=True)
    p = jnp.exp(s - m)
    l = jnp.sum(p, axis=0, keepdims=True)
    lane = lax.broadcasted_iota(jnp.int32, (1, LANES), 1)
    pn = p * jnp.where(lane < N_CROSS_HEADS, 1.0 / l, 0.0)
    wb = jnp.dot(pn.astype(BF16), expand_ref[...], preferred_element_type=F32)
    prod = wb * v_ref[...]
    part = jnp.sum(prod.reshape(N_MEM // SUBLANES, SUBLANES, D_MODEL), axis=0)
    o_ref[...] = jnp.sum(part, axis=0, keepdims=True).astype(o_ref.dtype)


def _cross_sample(qc3, mem_k_l, mem_v_l):
    Bs = qc3.shape[0]
    mk = mem_k_l.reshape(Bs, N_MEM, D_MODEL)
    mv = mem_v_l.reshape(Bs, N_MEM, D_MODEL)
    expand = np.zeros((LANES, D_MODEL), np.float32)
    for j in range(N_CROSS_HEADS):
        expand[j, j * CROSS_HD:(j + 1) * CROSS_HD] = 1.0
    expand = jnp.asarray(expand, BF16)
    blk = lambda b: (b, 0, 0)
    return pl.pallas_call(
        _cross_sample_kernel,
        grid=(Bs,),
        in_specs=[pl.BlockSpec((None, 1, D_MODEL), blk), _const_spec((LANES, D_MODEL)),
                  pl.BlockSpec((None, N_MEM, D_MODEL), blk), pl.BlockSpec((None, N_MEM, D_MODEL), blk)],
        out_specs=pl.BlockSpec((None, 1, D_MODEL), blk),
        out_shape=jax.ShapeDtypeStruct((Bs, 1, D_MODEL), BF16),
        compiler_params=_params(("arbitrary",)),
        name="cross_sample",
    )(qc3, expand, mk, mv)


def kernel(x_prompt, x_sample, mem_prompt, cache_k, cache_v, cache_mem_k, cache_mem_v, state_pool, state_conv, page_table, g_mix, w_in, pool_w, pool_scale, lam_q1, lam_k1, lam_q2, lam_k2, subln_g, w_out, g_cross, wq_c, wk_c, wv_c, wo_c, g_ffn, w_up, conv_w, conv_b, w_down, g_final):
    depth = w_in.shape[0]
    Bp, T, D = x_prompt.shape
    Bs = x_sample.shape[0]
    W2 = 2 * D_FF
    bf = lambda a: a.astype(BF16)
    w_in_b, pool_w_b, w_out_b = bf(w_in), bf(pool_w), bf(w_out)
    wq_b, wk_b, wv_b, wo_b, w_up_b, w_down_b = bf(wq_c), bf(wk_c), bf(wv_c), bf(wo_c), bf(w_up), bf(w_down)
    gfin = g_final.reshape(1, D)

    xp = x_prompt
    xs = x_sample.reshape(Bs, D)
    outs = {n: [] for n in ("kp", "vp", "mkp", "mvp", "pp", "cp", "ks", "vs", "ps", "cs")}
    for l in range(depth):
        lam0 = _lambda_init(l)
        last = l == depth - 1
        g_mix_l = g_mix[l].reshape(1, D)
        g_cross_l = g_cross[l].reshape(1, D)
        g_ffn_l = g_ffn[l].reshape(1, D)
        ps_l = pool_scale[l].reshape(1, POOL_WIDTH)
        cb_l = conv_b[l].reshape(1, W2)
        lamp = jnp.stack([lam_q1[l], lam_k1[l], lam_q2[l], lam_k2[l]])

        mk, mv, mkt, mvb = _mem_kv(mem_prompt, wk_b[l], wv_b[l])
        kp, vp, qt, kb, vt, po, tail = _in_proj_prompt(xp, g_mix_l, w_in_b[l], pool_w_b[l], ps_l)
        o = _flash_diff_attn(qt, kb, vt, lamp, subln_g[l].reshape(HEAD_W, 1), lam0)
        x1, qc = _out_q(xp.reshape(Bp * T, D), po.reshape(Bp * T, POOL_WIDTH), o.reshape(Bp * T, -1),
                        subln_g[l].reshape(1, HEAD_W), w_out_b[l], g_cross_l, wq_b[l], ROW_TILE)
        oc = _cross_prompt(qc.reshape(Bp, T, D), mkt, mvb, ROW_TILE)
        xp, up_tail = _ffn(x1.reshape(Bp, T, D), oc, None, wo_b[l], g_ffn_l, w_up_b[l], conv_w[l], cb_l,
                           w_down_b[l], gfin, ROW_TILE, last)
        outs["kp"].append(kp.reshape(Bp, T, N_HEADS, HEAD_W))
        outs["vp"].append(vp.reshape(Bp, T, N_HEADS, HEAD_W))
        outs["mkp"].append(mk.reshape(Bp, N_MEM, N_CROSS_HEADS, CROSS_HD))
        outs["mvp"].append(mv.reshape(Bp, N_MEM, N_CROSS_HEADS, CROSS_HD))
        outs["pp"].append(tail[:, 2 * SUBLANES - POOL_PAST:])
        outs["cp"].append(up_tail[:, SUBLANES - (CONV_W - 1):])

        st_t = jnp.swapaxes(state_pool[l], 0, 1)
        zs, pos, nst = _in_proj_sample(xs, g_mix_l, w_in_b[l], st_t, pool_w_b[l], ps_l)
        o_raw = _paged_attn(zs.reshape(Bs, 1, IN_WIDTH), page_table, cache_k[l], cache_v[l], lamp, lam0)
        x1s, qcs = _out_q(xs, pos, o_raw.reshape(Bs, -1), subln_g[l].reshape(1, HEAD_W), w_out_b[l],
                          g_cross_l, wq_b[l], Bs, subln_scale=1.0 - lam0)
        ocs = _cross_sample(qcs.reshape(Bs, 1, D), cache_mem_k[l], cache_mem_v[l])
        prev = (state_conv[l][:, 0].reshape(1, Bs, W2), state_conv[l][:, 1].reshape(1, Bs, W2))
        xs3, up_s = _ffn(x1s.reshape(1, Bs, D), ocs.reshape(1, Bs, D), prev, wo_b[l], g_ffn_l, w_up_b[l],
                         conv_w[l], cb_l, w_down_b[l], gfin, Bs, last)
        xs = xs3.reshape(Bs, D)
        outs["ks"].append(zs[:, POOL_WIDTH + QK_WIDTH:POOL_WIDTH + 2 * QK_WIDTH].reshape(Bs, 1, N_HEADS, HEAD_W))
        outs["vs"].append(zs[:, POOL_WIDTH + 2 * QK_WIDTH:].reshape(Bs, 1, N_HEADS, HEAD_W))
        outs["ps"].append(jnp.swapaxes(nst, 0, 1))
        outs["cs"].append(jnp.stack([state_conv[l][:, 1], up_s.reshape(Bs, W2)], axis=1))

    st = lambda n: jnp.stack(outs[n])
    return (xp, xs.reshape(Bs, 1, D), st("kp"), st("vp"), st("mkp"), st("mvp"), st("pp"), st("cp"),
            st("ks"), st("vs"), st("ps"), st("cs"))
```

```python
import functools
import math

import jax
import jax.numpy as jnp
import numpy as np
from jax import lax
from jax.experimental import pallas as pl
from jax.experimental.pallas import tpu as pltpu

F32 = jnp.float32
BF16 = jnp.bfloat16

D_MODEL = 1024
POOL_WIDTH = 512
POOL_WINDOWS = (2, 4, 8, 16)
POOL_GROUP = 128
POOL_PAST = 15
N_HEADS = 4
HEAD_W = 128
DIFF_HD = 64
QK_WIDTH = 512
IN_WIDTH = 2048
N_MEM = 256
N_CROSS_HEADS = 4
CROSS_HD = 256
D_FF = 2816
CONV_W = 3
EPS = 1e-6
SUBLN_EPS = 1e-5
PAGE = 128

LANES = 128
SUBLANES = 8
VMEM_LIMIT = 56 * 1024 * 1024

ROW_TILE = 512
TQ = 256
TK = 256
FF_CHUNKS = ((0, 1024), (1024, 1024), (2048, 768))
NEG_INF = float("-inf")


def _lambda_init(layer_idx):
    return 0.8 - 0.6 * math.exp(-0.3 * layer_idx)


def _rms(x, g, eps):
    ms = jnp.mean(x * x, axis=-1, keepdims=True)
    return x * lax.rsqrt(ms + eps) * g


def _params(sem, vmem=VMEM_LIMIT):
    return pltpu.CompilerParams(dimension_semantics=sem, vmem_limit_bytes=vmem)


def _const_spec(shape):
    nd = len(shape)
    return pl.BlockSpec(shape, lambda *_: (0,) * nd)


def _pool_groups(u, shifted, cnt_fn, pw_ref, ps_ref, po_ref):
    for g, w in enumerate(POOL_WINDOWS):
        cols = slice(g * POOL_GROUP, (g + 1) * POOL_GROUP)
        ug = u[:, cols]
        ws = ug
        for j in range(1, w):
            ws = ws + shifted(j, cols)
        p = ws / cnt_fn(w) - ug
        y = jnp.dot(p.astype(BF16), pw_ref[g], preferred_element_type=F32)
        po_ref[:, cols] = (y * ps_ref[:, cols]).astype(po_ref.dtype)


def _in_proj_prompt_kernel(x_ref, g_ref, w_ref, pw_ref, ps_ref,
                           k_ref, v_ref, qt_ref, kb_ref, vt_ref, po_ref, tail_ref, ext_ref, *, tm):
    t = pl.program_id(1)
    halo = 2 * SUBLANES

    @pl.when(t == 0)
    def _():
        ext_ref[0:halo, :] = jnp.zeros((halo, POOL_WIDTH), F32)

    h = _rms(x_ref[...], g_ref[...], EPS).astype(BF16)
    z = jnp.dot(h, w_ref[...], preferred_element_type=F32)
    u = z[:, :POOL_WIDTH]
    k_ref[...] = z[:, POOL_WIDTH + QK_WIDTH:POOL_WIDTH + 2 * QK_WIDTH]
    v_ref[...] = z[:, POOL_WIDTH + 2 * QK_WIDTH:]
    scale = DIFF_HD ** -0.5
    for hh in range(N_HEADS):
        q0 = POOL_WIDTH + hh * HEAD_W
        k0 = POOL_WIDTH + QK_WIDTH + hh * HEAD_W
        v0 = POOL_WIDTH + 2 * QK_WIDTH + hh * HEAD_W
        for s in range(tm // TQ):
            rows = slice(s * TQ, (s + 1) * TQ)
            qt_ref[hh, s] = (z[rows, q0:q0 + HEAD_W] * scale).T.astype(BF16)
            vt_ref[hh, s] = z[rows, v0:v0 + HEAD_W].T.astype(BF16)
            kb_ref[hh, s] = z[rows, k0:k0 + HEAD_W].astype(BF16)

    ext_ref[halo:halo + tm, :] = u
    pos = t * tm + lax.broadcasted_iota(jnp.int32, (tm, POOL_GROUP), 0)

    def shifted(j, cols):
        return ext_ref[halo - j:halo - j + tm, cols]

    def cnt(w):
        return jnp.minimum(pos + 1, w).astype(F32)

    _pool_groups(u, shifted, cnt, pw_ref, ps_ref, po_ref)
    tail_ref[...] = u[tm - halo:, :]
    ext_ref[0:halo, :] = u[tm - halo:, :]


def _in_proj_prompt(x, g, w_in, pool_w, pool_scale):
    B, T, D = x.shape
    tm = ROW_TILE
    nt = T // tm
    spt = tm // TQ
    row = lambda b, t: (b, t, 0)
    out_shape = (
        jax.ShapeDtypeStruct((B, T, QK_WIDTH), F32),
        jax.ShapeDtypeStruct((B, T, QK_WIDTH), F32),
        jax.ShapeDtypeStruct((B, N_HEADS, T // TQ, HEAD_W, TQ), BF16),
        jax.ShapeDtypeStruct((B, N_HEADS, T // TK, TK, HEAD_W), BF16),
        jax.ShapeDtypeStruct((B, N_HEADS, T // TK, HEAD_W, TK), BF16),
        jax.ShapeDtypeStruct((B, T, POOL_WIDTH), BF16),
        jax.ShapeDtypeStruct((B, 2 * SUBLANES, POOL_WIDTH), F32),
    )
    tile5 = lambda b, t: (b, 0, t, 0, 0)
    return pl.pallas_call(
        functools.partial(_in_proj_prompt_kernel, tm=tm),
        grid=(B, nt),
        in_specs=[
            pl.BlockSpec((None, tm, D), row),
            _const_spec((1, D)),
            _const_spec((D, IN_WIDTH)),
            _const_spec((len(POOL_WINDOWS), POOL_GROUP, POOL_GROUP)),
            _const_spec((1, POOL_WIDTH)),
        ],
        out_specs=(
            pl.BlockSpec((None, tm, QK_WIDTH), row),
            pl.BlockSpec((None, tm, QK_WIDTH), row),
            pl.BlockSpec((None, N_HEADS, spt, HEAD_W, TQ), tile5),
            pl.BlockSpec((None, N_HEADS, spt, TK, HEAD_W), tile5),
            pl.BlockSpec((None, N_HEADS, spt, HEAD_W, TK), tile5),
            pl.BlockSpec((None, tm, POOL_WIDTH), row),
            pl.BlockSpec((None, 2 * SUBLANES, POOL_WIDTH), lambda b, t: (b, 0, 0)),
        ),
        out_shape=out_shape,
        scratch_shapes=[pltpu.VMEM((tm + 2 * SUBLANES, POOL_WIDTH), F32)],
        compiler_params=_params(("arbitrary", "arbitrary")),
        name="in_proj_prompt",
    )(x, g, w_in, pool_w, pool_scale)


def _lambda_value(lamp_ref, lam0):
    lp = lamp_ref[...]
    s1 = jnp.sum(lp[0:1] * lp[1:2], axis=1, keepdims=True)
    s2 = jnp.sum(lp[2:3] * lp[3:4], axis=1, keepdims=True)
    return jnp.exp(s1) - jnp.exp(s2) + lam0


def _flash_kernel(lamp_ref, g_ref, qt_ref, k_ref, vt_ref, o_ref, rhs_ref, m_ref, acc_ref, *, lam0):
    i = pl.program_id(1)
    two_tq = 2 * TQ
    slopes = [2.0 ** (-8.0 * (hh + 1) / N_HEADS) for hh in range(N_HEADS)]

    z64 = jnp.zeros((DIFF_HD, TQ), BF16)
    row = lax.broadcasted_iota(jnp.int32, (HEAD_W, two_tq), 0)
    col = lax.broadcasted_iota(jnp.int32, (HEAD_W, two_tq), 1)
    r_f = (col & (TQ - 1)).astype(F32)
    for hh in range(N_HEADS):
        qt = qt_ref[hh]
        top = jnp.concatenate([qt[:DIFF_HD], z64], axis=1)
        mid = jnp.concatenate([z64, qt[DIFF_HD:]], axis=1)
        bot = jnp.where(row == 0, -slopes[hh] * r_f, jnp.where(row == 1, slopes[hh], 0.0)).astype(BF16)
        rhs_ref[hh] = jnp.concatenate([top, mid, bot], axis=0)
    m_ref[...] = jnp.full(m_ref.shape, NEG_INF, F32)
    acc_ref[...] = jnp.zeros(acc_ref.shape, F32)

    krow = lax.broadcasted_iota(jnp.int32, (TK, HEAD_W), 0)
    klane = lax.broadcasted_iota(jnp.int32, (TK, HEAD_W), 1)
    aug_k = jnp.where(klane == 0, 1.0, jnp.where(klane == 1, krow.astype(F32), 0.0)).astype(BF16)
    ones_rows = jnp.ones((2 * SUBLANES, TK), BF16)

    def step(j, masked):
        tile_dist = ((i - j) * TQ).astype(F32)
        scores = []
        for hh in range(N_HEADS):
            kk = jnp.concatenate([k_ref[hh, j], aug_k], axis=1)
            s = jnp.dot(kk, rhs_ref[hh], preferred_element_type=F32)
            if masked:
                c_idx = lax.broadcasted_iota(jnp.int32, (TK, two_tq), 0)
                r_idx = lax.broadcasted_iota(jnp.int32, (TK, two_tq), 1) & (TQ - 1)
                s = jnp.where(c_idx <= r_idx, s, NEG_INF)
            scores.append(s)
        probs = []
        for hh in range(N_HEADS):
            s = scores[hh]
            dlt = slopes[hh] * tile_dist
            m_old = m_ref[hh]
            m_new = jnp.maximum(m_old, jnp.max(s, axis=0, keepdims=True) - dlt)
            probs.append((jnp.exp(s - (m_new + dlt)).astype(BF16), jnp.exp(m_old - m_new)))
            m_ref[hh] = m_new
        for hh in range(N_HEADS):
            p, alpha = probs[hh]
            v1 = jnp.concatenate([vt_ref[hh, j], ones_rows], axis=0)
            pv = jnp.dot(v1, p, preferred_element_type=F32)
            acc_ref[hh] = alpha * acc_ref[hh] + pv

    def body(j, carry):
        step(j, False)
        return carry

    lax.fori_loop(0, i, body, 0)
    step(i, True)

    lam = _lambda_value(lamp_ref, lam0)
    for hh in range(N_HEADS):
        acc = acc_ref[hh]
        o1 = acc[:HEAD_W, :TQ] / acc[HEAD_W:HEAD_W + 1, :TQ]
        o2 = acc[:HEAD_W, TQ:] / acc[HEAD_W:HEAD_W + 1, TQ:]
        ot = o1 - lam * o2
        ms = jnp.mean(ot * ot, axis=0, keepdims=True)
        y = ot * lax.rsqrt(ms + SUBLN_EPS) * g_ref[...] * (1.0 - lam0)
        o_ref[:, hh * HEAD_W:(hh + 1) * HEAD_W] = y.T.astype(o_ref.dtype)


def _flash_diff_attn(qt, kb, vt, lamp, subln_g, lam0):
    B, H, nq = qt.shape[0], qt.shape[1], qt.shape[2]
    nk = kb.shape[2]
    T = nq * TQ
    whole = lambda b, i: (b, 0, 0, 0, 0)
    return pl.pallas_call(
        functools.partial(_flash_kernel, lam0=lam0),
        grid=(B, nq),
        in_specs=[
            _const_spec((4, DIFF_HD)),
            _const_spec((HEAD_W, 1)),
            pl.BlockSpec((None, H, None, HEAD_W, TQ), lambda b, i: (b, 0, i, 0, 0)),
            pl.BlockSpec((None, H, nk, TK, HEAD_W), whole, pipeline_mode=pl.Buffered(1)),
            pl.BlockSpec((None, H, nk, HEAD_W, TK), whole, pipeline_mode=pl.Buffered(1)),
        ],
        out_specs=pl.BlockSpec((None, TQ, H * HEAD_W), lambda b, i: (b, i, 0)),
        out_shape=jax.ShapeDtypeStruct((B, T, H * HEAD_W), BF16),
        scratch_shapes=[
            pltpu.VMEM((H, 2 * HEAD_W, 2 * TQ), BF16),
            pltpu.VMEM((H, 1, 2 * TQ), F32),
            pltpu.VMEM((H, HEAD_W + 2 * SUBLANES, 2 * TQ), F32),
        ],
        compiler_params=_params(("arbitrary", "arbitrary")),
        name="flash_diff_attn",
    )(lamp, subln_g, qt, kb, vt)


def _out_q_kernel(x_ref, po_ref, o_ref, sg_ref, wout_ref, gc_ref, wq_ref, x1_ref, qc_ref, *, subln_scale):
    o = o_ref[...]
    if subln_scale is not None:
        parts = []
        for hh in range(N_HEADS):
            oh = o[:, hh * HEAD_W:(hh + 1) * HEAD_W]
            parts.append(_rms(oh, sg_ref[...], SUBLN_EPS) * subln_scale)
        o = jnp.concatenate(parts, axis=1)
    mix = jnp.dot(po_ref[...], wout_ref[0:POOL_WIDTH, :], preferred_element_type=F32)
    mix = mix + jnp.dot(o.astype(BF16), wout_ref[POOL_WIDTH:, :], preferred_element_type=F32)
    x1 = x_ref[...] + mix
    x1_ref[...] = x1
    hc = _rms(x1, gc_ref[...], EPS).astype(BF16)
    qc = jnp.dot(hc, wq_ref[...], preferred_element_type=F32) * (CROSS_HD ** -0.5)
    qc_ref[...] = qc.astype(BF16)


def _out_q(x, po, o, subln_g_row, w_out, g_cross, wq_c, tm, subln_scale=None):
    M, D = x.shape
    row = lambda t: (t, 0)
    return pl.pallas_call(
        functools.partial(_out_q_kernel, subln_scale=subln_scale),
        grid=(M // tm,),
        in_specs=[
            pl.BlockSpec((tm, D), row),
            pl.BlockSpec((tm, POOL_WIDTH), row),
            pl.BlockSpec((tm, N_HEADS * HEAD_W), row),
            _const_spec((1, HEAD_W)),
            _const_spec((D, D)),
            _const_spec((1, D)),
            _const_spec((D, D)),
        ],
        out_specs=(pl.BlockSpec((tm, D), row), pl.BlockSpec((tm, D), row)),
        out_shape=(jax.ShapeDtypeStruct((M, D), F32), jax.ShapeDtypeStruct((M, D), BF16)),
        compiler_params=_params(("arbitrary",)),
        name="out_q",
    )(x, po, o, subln_g_row, w_out, g_cross, wq_c)


def _mem_kv_kernel(mem_ref, wk_ref, wv_ref, mk_ref, mv_ref, mkt_ref, mvb_ref):
    m = mem_ref[...].astype(BF16)
    k = jnp.dot(m, wk_ref[...], preferred_element_type=F32)
    v = jnp.dot(m, wv_ref[...], preferred_element_type=F32)
    mk_ref[...] = k
    mv_ref[...] = v
    mkt_ref[...] = k.T.astype(BF16)
    mvb_ref[...] = v.astype(BF16)


def _mem_kv(mem, wk, wv):
    B, N, D = mem.shape
    blk = lambda b: (b, 0, 0)
    return pl.pallas_call(
        _mem_kv_kernel,
        grid=(B,),
        in_specs=[pl.BlockSpec((None, N, D), blk), _const_spec((D, D)), _const_spec((D, D))],
        out_specs=(pl.BlockSpec((None, N, D), blk), pl.BlockSpec((None, N, D), blk),
                   pl.BlockSpec((None, D, N), blk), pl.BlockSpec((None, N, D), blk)),
        out_shape=(jax.ShapeDtypeStruct((B, N, D), F32), jax.ShapeDtypeStruct((B, N, D), F32),
                   jax.ShapeDtypeStruct((B, D, N), BF16), jax.ShapeDtypeStruct((B, N, D), BF16)),
        compiler_params=_params(("arbitrary",)),
        name="mem_kv",
    )(mem, wk, wv)


def _cross_prompt_kernel(q_ref, kt_ref, v_ref, o_ref):
    for hh in range(N_CROSS_HEADS):
        cols = slice(hh * CROSS_HD, (hh + 1) * CROSS_HD)
        s = jnp.dot(q_ref[:, cols], kt_ref[cols, :], preferred_element_type=F32)
        m = jnp.max(s, axis=1, keepdims=True)
        p = jnp.exp(s - m)
        l = jnp.sum(p, axis=1, keepdims=True)
        o = jnp.dot(p.astype(BF16), v_ref[:, cols], preferred_element_type=F32) / l
        o_ref[:, cols] = o.astype(o_ref.dtype)


def _cross_prompt(qc, mkt, mvb, tm):
    B, T, D = qc.shape
    row = lambda b, t: (b, t, 0)
    blk = lambda b, t: (b, 0, 0)
    return pl.pallas_call(
        _cross_prompt_kernel,
        grid=(B, T // tm),
        in_specs=[pl.BlockSpec((None, tm, D), row),
                  pl.BlockSpec((None, D, N_MEM), blk),
                  pl.BlockSpec((None, N_MEM, D), blk)],
        out_specs=pl.BlockSpec((None, tm, D), row),
        out_shape=jax.ShapeDtypeStruct((B, T, D), BF16),
        compiler_params=_params(("arbitrary", "arbitrary")),
        name="cross_prompt",
    )(qc, mkt, mvb)


def _ffn_kernel(*refs, seq_mode, final_norm, tm):
    if seq_mode:
        (x1_ref, oc_ref, wo_ref, g_ref, wup_ref, cw_ref, cb_ref, wdn_ref, gfin_ref,
         out_ref, up_out_ref, acc_ref, h_ref, ext_ref, halo_ref) = refs
    else:
        (x1_ref, oc_ref, p2_ref, p1_ref, wo_ref, g_ref, wup_ref, cw_ref, cb_ref, wdn_ref, gfin_ref,
         out_ref, up_out_ref, acc_ref, h_ref) = refs
    t = pl.program_id(1)

    if seq_mode:
        @pl.when(t == 0)
        def _():
            halo_ref[...] = jnp.zeros(halo_ref.shape, F32)

    x2 = x1_ref[...] + jnp.dot(oc_ref[...], wo_ref[...], preferred_element_type=F32)
    acc_ref[...] = x2
    h_ref[...] = _rms(x2, g_ref[...], EPS).astype(BF16)

    def conv_part(col0, cw):
        cols = slice(col0, col0 + cw)
        up = jnp.dot(h_ref[...], wup_ref[:, cols], preferred_element_type=F32)
        if seq_mode:
            ext_ref[0:SUBLANES, 0:cw] = halo_ref[:, cols]
            ext_ref[SUBLANES:SUBLANES + tm, 0:cw] = up
            um1 = ext_ref[SUBLANES - 1:SUBLANES - 1 + tm, 0:cw]
            um2 = ext_ref[SUBLANES - 2:SUBLANES - 2 + tm, 0:cw]
            halo_ref[:, cols] = up[tm - SUBLANES:, :]
            up_out_ref[:, cols] = up[tm - SUBLANES:, :]
        else:
            um1 = p1_ref[:, cols]
            um2 = p2_ref[:, cols]
            up_out_ref[:, cols] = up
        c = cb_ref[:, cols] + um2 * cw_ref[0:1, cols]
        c = c + um1 * cw_ref[1:2, cols]
        return c + up * cw_ref[2:3, cols]

    for c0, cw in FF_CHUNKS:
        a = conv_part(c0, cw)
        g = conv_part(D_FF + c0, cw)
        act = (g * (1.0 / (1.0 + jnp.exp(-g))) * a).astype(BF16)
        acc_ref[...] += jnp.dot(act, wdn_ref[c0:c0 + cw, :], preferred_element_type=F32)

    x3 = acc_ref[...]
    if final_norm:
        x3 = _rms(x3, gfin_ref[...], EPS)
    out_ref[...] = x3


def _ffn(x1, oc, prev, wo, g_ffn, w_up, conv_w, conv_b, w_down, g_final, tm, final_norm):
    B, T, D = x1.shape
    seq_mode = prev is None
    W2 = 2 * D_FF
    row = lambda b, t: (b, t, 0)
    max_cw = max(cw for _, cw in FF_CHUNKS)
    in_specs = [pl.BlockSpec((None, tm, D), row), pl.BlockSpec((None, tm, D), row)]
    args = [x1, oc]
    if not seq_mode:
        in_specs += [pl.BlockSpec((None, tm, W2), row), pl.BlockSpec((None, tm, W2), row)]
        args += list(prev)
    in_specs += [_const_spec((D, D)), _const_spec((1, D)), _const_spec((D, W2)),
                 _const_spec((CONV_W, W2)), _const_spec((1, W2)), _const_spec((D_FF, D)),
                 _const_spec((1, D))]
    args += [wo, g_ffn, w_up, conv_w, conv_b, w_down, g_final]
    scratch = [pltpu.VMEM((tm, D), F32), pltpu.VMEM((tm, D), BF16)]
    if seq_mode:
        up_rows = SUBLANES
        up_spec = pl.BlockSpec((None, SUBLANES, W2), lambda b, t: (b, 0, 0))
        scratch += [pltpu.VMEM((tm + SUBLANES, max_cw), F32), pltpu.VMEM((SUBLANES, W2), F32)]
    else:
        up_rows = T
        up_spec = pl.BlockSpec((None, tm, W2), row)
    return pl.pallas_call(
        functools.partial(_ffn_kernel, seq_mode=seq_mode, final_norm=final_norm, tm=tm),
        grid=(B, T // tm),
        in_specs=in_specs,
        out_specs=(pl.BlockSpec((None, tm, D), row), up_spec),
        out_shape=(jax.ShapeDtypeStruct((B, T, D), F32), jax.ShapeDtypeStruct((B, up_rows, W2), F32)),
        scratch_shapes=scratch,
        compiler_params=_params(("arbitrary", "arbitrary")),
        name="wo_ffn_seq" if seq_mode else "wo_ffn_step",
    )(*args)


def _in_proj_sample_kernel(x_ref, g_ref, w_ref, st_ref, pw_ref, ps_ref, z_ref, po_ref, nst_ref):
    h = _rms(x_ref[...], g_ref[...], EPS).astype(BF16)
    z = jnp.dot(h, w_ref[...], preferred_element_type=F32)
    z_ref[...] = z
    u = z[:, :POOL_WIDTH]

    def shifted(j, cols):
        return st_ref[POOL_PAST - j, :, cols]

    _pool_groups(u, shifted, lambda w: float(w), pw_ref, ps_ref, po_ref)
    for r in range(POOL_PAST - 1):
        nst_ref[r] = st_ref[r + 1]
    nst_ref[POOL_PAST - 1] = u


def _in_proj_sample(x, g, w_in, st_t, pool_w, pool_scale):
    M, D = x.shape
    return pl.pallas_call(
        _in_proj_sample_kernel,
        grid=(1,),
        in_specs=[_const_spec((M, D)), _const_spec((1, D)), _const_spec((D, IN_WIDTH)),
                  _const_spec((POOL_PAST, M, POOL_WIDTH)),
                  _const_spec((len(POOL_WINDOWS), POOL_GROUP, POOL_GROUP)),
                  _const_spec((1, POOL_WIDTH))],
        out_specs=(_const_spec((M, IN_WIDTH)), _const_spec((M, POOL_WIDTH)),
                   _const_spec((POOL_PAST, M, POOL_WIDTH))),
        out_shape=(jax.ShapeDtypeStruct((M, IN_WIDTH), F32),
                   jax.ShapeDtypeStruct((M, POOL_WIDTH), BF16),
                   jax.ShapeDtypeStruct((POOL_PAST, M, POOL_WIDTH), F32)),
        compiler_params=_params(("arbitrary",)),
        name="in_proj_sample",
    )(x, g, w_in, st_t, pool_w, pool_scale)


def _paged_attn_kernel(pt_ref, lamp_ref, q_ref, kn_ref, vn_ref, bias_ref, *rest, n_pages, lam0):
    del pt_ref
    k_refs = rest[:n_pages]
    v_refs = rest[n_pages:2 * n_pages]
    o_ref = rest[2 * n_pages]
    s_ref = rest[2 * n_pages + 1]
    rows = PAGE * N_HEADS
    nt_dims = (((1,), (1,)), ((), ()))

    q4 = q_ref[...] * (DIFF_HD ** -0.5)
    lane = lax.broadcasted_iota(jnp.int32, (N_HEADS, HEAD_W), 1)
    q_all = jnp.concatenate([jnp.where(lane < DIFF_HD, q4, 0.0),
                             jnp.where(lane >= DIFF_HD, q4, 0.0)], axis=0)
    q_all_b = q_all.astype(BF16)
    for p in range(n_pages):
        kp = k_refs[p][...].astype(BF16)
        s_ref[:, p * rows:(p + 1) * rows] = lax.dot_general(q_all_b, kp, nt_dims, preferred_element_type=F32)

    s = s_ref[...] + bias_ref[...]
    kn8 = jnp.concatenate([kn_ref[...], kn_ref[...]], axis=0)
    s_new = jnp.sum(q_all * kn8, axis=1, keepdims=True)
    m = jnp.maximum(jnp.max(s, axis=1, keepdims=True), s_new)
    pr = jnp.exp(s - m)
    pr_new = jnp.exp(s_new - m)
    l = jnp.sum(pr, axis=1, keepdims=True) + pr_new
    lam = _lambda_value(lamp_ref, lam0)
    rowi = lax.broadcasted_iota(jnp.int32, (2 * N_HEADS, 1), 0)
    coef = jnp.where(rowi < N_HEADS, 1.0, -lam) / l
    w = (pr * coef).astype(BF16)

    vn8 = jnp.concatenate([vn_ref[...], vn_ref[...]], axis=0)
    acc = (pr_new * coef) * vn8
    for p in range(n_pages):
        vp = v_refs[p][...].astype(BF16)
        acc = acc + jnp.dot(w[:, p * rows:(p + 1) * rows], vp, preferred_element_type=F32)
    o_ref[...] = acc[:N_HEADS] + acc[N_HEADS:]


def _paged_attn(zs, page_table, cache_k, cache_v, layer, lamp, lam0):
    Bs = zs.shape[0]
    n_pages = page_table.shape[1]
    n_past = n_pages * PAGE
    depth, n_phys = cache_k.shape[0], cache_k.shape[1]
    rows = PAGE * N_HEADS
    ck = cache_k.reshape(depth * n_phys, rows, HEAD_W)
    cv = cache_v.reshape(depth * n_phys, rows, HEAD_W)
    heads3 = lambda lo: zs[:, lo:lo + QK_WIDTH].reshape(Bs, N_HEADS, HEAD_W)
    q3 = heads3(POOL_WIDTH)
    kn3 = heads3(POOL_WIDTH + QK_WIDTH)
    vn3 = heads3(POOL_WIDTH + 2 * QK_WIDTH)

    slopes = 2.0 ** (-8.0 * np.arange(1, N_HEADS + 1) / N_HEADS)
    key_row = np.arange(n_past * N_HEADS)
    key_pos, key_head = key_row // N_HEADS, key_row % N_HEADS
    row_head = np.arange(2 * N_HEADS) % N_HEADS
    alibi = -slopes[row_head][:, None] * (n_past - key_pos)[None, :].astype(np.float64)
    bias = jnp.asarray(np.where(row_head[:, None] == key_head[None, :], alibi, -np.inf), F32)

    def page_spec(p):
        return pl.BlockSpec((None, rows, HEAD_W), lambda b, pt: (layer * n_phys + pt[b, p], 0, 0))

    const2 = lambda b, pt: (0, 0)
    row3 = pl.BlockSpec((None, N_HEADS, HEAD_W), lambda b, pt: (b, 0, 0))
    grid_spec = pltpu.PrefetchScalarGridSpec(
        num_scalar_prefetch=1,
        grid=(Bs,),
        in_specs=[pl.BlockSpec((4, DIFF_HD), const2), row3, row3, row3,
                  pl.BlockSpec((2 * N_HEADS, n_past * N_HEADS), const2)]
                 + [page_spec(p) for p in range(n_pages)]
                 + [page_spec(p) for p in range(n_pages)],
        out_specs=row3,
        scratch_shapes=[pltpu.VMEM((2 * N_HEADS, n_past * N_HEADS), F32)],
    )
    return pl.pallas_call(
        functools.partial(_paged_attn_kernel, n_pages=n_pages, lam0=lam0),
        grid_spec=grid_spec,
        out_shape=jax.ShapeDtypeStruct((Bs, N_HEADS, HEAD_W), F32),
        compiler_params=_params(("arbitrary",)),
        name="paged_diff_attn",
    )(page_table, lamp, q3, kn3, vn3, bias, *([ck] * n_pages), *([cv] * n_pages))


def _cross_sample_kernel(q_ref, expand_ref, k_ref, v_ref, o_ref):
    q = q_ref[...].astype(F32)
    rj = lax.broadcasted_iota(jnp.int32, (LANES, D_MODEL), 0)
    cj = lax.broadcasted_iota(jnp.int32, (LANES, D_MODEL), 1) >> 8
    q_blk = jnp.where(rj == cj, jnp.broadcast_to(q, (LANES, D_MODEL)), 0.0).astype(BF16)

    def mem_rows(ref):
        return jnp.concatenate([ref[:, hh, :] for hh in range(N_CROSS_HEADS)], axis=1)

    s = lax.dot_general(mem_rows(k_ref).astype(BF16), q_blk, (((1,), (1,)), ((), ())),
                        preferred_element_type=F32)
    m = jnp.max(s, axis=0, keepdims=True)
    p = jnp.exp(s - m)
    l = jnp.sum(p, axis=0, keepdims=True)
    lane = lax.broadcasted_iota(jnp.int32, (1, LANES), 1)
    pn = p * jnp.where(lane < N_CROSS_HEADS, 1.0 / l, 0.0)
    wb = jnp.dot(pn.astype(BF16), expand_ref[...], preferred_element_type=F32)
    prod = wb * mem_rows(v_ref)
    part = jnp.sum(prod.reshape(N_MEM // SUBLANES, SUBLANES, D_MODEL), axis=0)
    o_ref[...] = jnp.sum(part, axis=0, keepdims=True).astype(o_ref.dtype)


def _cross_sample(qc3, mem_k, mem_v, layer):
    Bs = qc3.shape[0]
    mem_blk = pl.BlockSpec((None, None, N_MEM, N_CROSS_HEADS, CROSS_HD), lambda b: (layer, b, 0, 0, 0))
    expand = np.zeros((LANES, D_MODEL), np.float32)
    for j in range(N_CROSS_HEADS):
        expand[j, j * CROSS_HD:(j + 1) * CROSS_HD] = 1.0
    expand = jnp.asarray(expand, BF16)
    blk = lambda b: (b, 0, 0)
    return pl.pallas_call(
        _cross_sample_kernel,
        grid=(Bs,),
        in_specs=[pl.BlockSpec((None, 1, D_MODEL), blk), _const_spec((LANES, D_MODEL)), mem_blk, mem_blk],
        out_specs=pl.BlockSpec((None, 1, D_MODEL), blk),
        out_shape=jax.ShapeDtypeStruct((Bs, 1, D_MODEL), BF16),
        compiler_params=_params(("arbitrary",)),
        name="cross_sample",
    )(qc3, expand, mem_k, mem_v)


def kernel(x_prompt, x_sample, mem_prompt, cache_k, cache_v, cache_mem_k, cache_mem_v, state_pool, state_conv, page_table, g_mix, w_in, pool_w, pool_scale, lam_q1, lam_k1, lam_q2, lam_k2, subln_g, w_out, g_cross, wq_c, wk_c, wv_c, wo_c, g_ffn, w_up, conv_w, conv_b, w_down, g_final):
    depth = w_in.shape[0]
    Bp, T, D = x_prompt.shape
    Bs = x_sample.shape[0]
    W2 = 2 * D_FF
    bf = lambda a: a.astype(BF16)
    w_in_b, pool_w_b, w_out_b = bf(w_in), bf(pool_w), bf(w_out)
    wq_b, wk_b, wv_b, wo_b, w_up_b, w_down_b = bf(wq_c), bf(wk_c), bf(wv_c), bf(wo_c), bf(w_up), bf(w_down)
    gfin = g_final.reshape(1, D)

    xp = x_prompt
    xs = x_sample.reshape(Bs, D)
    outs = {n: [] for n in ("kp", "vp", "mkp", "mvp", "pp", "cp", "ks", "vs", "ps", "cs")}
    for l in range(depth):
        lam0 = _lambda_init(l)
        last = l == depth - 1
        g_mix_l = g_mix[l].reshape(1, D)
        g_cross_l = g_cross[l].reshape(1, D)
        g_ffn_l = g_ffn[l].reshape(1, D)
        ps_l = pool_scale[l].reshape(1, POOL_WIDTH)
        cb_l = conv_b[l].reshape(1, W2)
        lamp = jnp.stack([lam_q1[l], lam_k1[l], lam_q2[l], lam_k2[l]])

        mk, mv, mkt, mvb = _mem_kv(mem_prompt, wk_b[l], wv_b[l])
        kp, vp, qt, kb, vt, po, tail = _in_proj_prompt(xp, g_mix_l, w_in_b[l], pool_w_b[l], ps_l)
        o = _flash_diff_attn(qt, kb, vt, lamp, subln_g[l].reshape(HEAD_W, 1), lam0)
        x1, qc = _out_q(xp.reshape(Bp * T, D), po.reshape(Bp * T, POOL_WIDTH), o.reshape(Bp * T, -1),
                        subln_g[l].reshape(1, HEAD_W), w_out_b[l], g_cross_l, wq_b[l], ROW_TILE)
        oc = _cross_prompt(qc.reshape(Bp, T, D), mkt, mvb, ROW_TILE)
        xp, up_tail = _ffn(x1.reshape(Bp, T, D), oc, None, wo_b[l], g_ffn_l, w_up_b[l], conv_w[l], cb_l,
                           w_down_b[l], gfin, ROW_TILE, last)
        outs["kp"].append(kp.reshape(Bp, T, N_HEADS, HEAD_W))
        outs["vp"].append(vp.reshape(Bp, T, N_HEADS, HEAD_W))
        outs["mkp"].append(mk.reshape(Bp, N_MEM, N_CROSS_HEADS, CROSS_HD))
        outs["mvp"].append(mv.reshape(Bp, N_MEM, N_CROSS_HEADS, CROSS_HD))
        outs["pp"].append(tail[:, 2 * SUBLANES - POOL_PAST:])
        outs["cp"].append(up_tail[:, SUBLANES - (CONV_W - 1):])

        st_t = jnp.swapaxes(state_pool[l], 0, 1)
        zs, pos, nst = _in_proj_sample(xs, g_mix_l, w_in_b[l], st_t, pool_w_b[l], ps_l)
        o_raw = _paged_attn(zs, page_table, cache_k, cache_v, l, lamp, lam0)
        x1s, qcs = _out_q(xs, pos, o_raw.reshape(Bs, -1), subln_g[l].reshape(1, HEAD_W), w_out_b[l],
                          g_cross_l, wq_b[l], Bs, subln_scale=1.0 - lam0)
        ocs = _cross_sample(qcs.reshape(Bs, 1, D), cache_mem_k, cache_mem_v, l)
        prev = (state_conv[l][:, 0].reshape(1, Bs, W2), state_conv[l][:, 1].reshape(1, Bs, W2))
        xs3, up_s = _ffn(x1s.reshape(1, Bs, D), ocs.reshape(1, Bs, D), prev, wo_b[l], g_ffn_l, w_up_b[l],
                         conv_w[l], cb_l, w_down_b[l], gfin, Bs, last)
        xs = xs3.reshape(Bs, D)
        outs["ks"].append(zs[:, POOL_WIDTH + QK_WIDTH:POOL_WIDTH + 2 * QK_WIDTH].reshape(Bs, 1, N_HEADS, HEAD_W))
        outs["vs"].append(zs[:, POOL_WIDTH + 2 * QK_WIDTH:].reshape(Bs, 1, N_HEADS, HEAD_W))
        outs["ps"].append(jnp.swapaxes(nst, 0, 1))
        outs["cs"].append(jnp.stack([state_conv[l][:, 1], up_s.reshape(Bs, W2)], axis=1))

    st = lambda n: jnp.stack(outs[n])
    return (xp, xs.reshape(Bs, 1, D), st("kp"), st("vp"), st("mkp"), st("mvp"), st("pp"), st("cp"),
            st("ks"), st("vs"), st("ps"), st("cs"))
```

```python
import functools
import math

import jax
import jax.numpy as jnp
import numpy as np
from jax import lax
from jax.experimental import pallas as pl
from jax.experimental.pallas import tpu as pltpu

F32 = jnp.float32
BF16 = jnp.bfloat16

D_MODEL = 1024
POOL_WIDTH = 512
POOL_WINDOWS = (2, 4, 8, 16)
POOL_GROUP = 128
POOL_PAST = 15
N_HEADS = 4
HEAD_W = 128
DIFF_HD = 64
QK_WIDTH = 512
IN_WIDTH = 2048
N_MEM = 256
N_CROSS_HEADS = 4
CROSS_HD = 256
D_FF = 2816
CONV_W = 3
EPS = 1e-6
SUBLN_EPS = 1e-5
PAGE = 128

LANES = 128
SUBLANES = 8
VMEM_LIMIT = 56 * 1024 * 1024

ROW_TILE = 512
TQ = 256
TK = 256
FF_CHUNK = 256
NEG_INF = float("-inf")


def _lambda_init(layer_idx):
    return 0.8 - 0.6 * math.exp(-0.3 * layer_idx)


def _rms(x, g, eps):
    ms = jnp.mean(x * x, axis=-1, keepdims=True)
    return x * lax.rsqrt(ms + eps) * g


def _params(sem, vmem=VMEM_LIMIT):
    return pltpu.CompilerParams(dimension_semantics=sem, vmem_limit_bytes=vmem)


def _const_spec(shape):
    nd = len(shape)
    return pl.BlockSpec(shape, lambda *_: (0,) * nd)


def _pool_groups(u, shifted, cnt_fn, pw_ref, ps_ref, po_ref):
    for g, w in enumerate(POOL_WINDOWS):
        cols = slice(g * POOL_GROUP, (g + 1) * POOL_GROUP)
        ug = u[:, cols]
        ws = ug
        for j in range(1, w):
            ws = ws + shifted(j, cols)
        p = ws / cnt_fn(w) - ug
        y = jnp.dot(p.astype(BF16), pw_ref[g], preferred_element_type=F32)
        po_ref[:, cols] = (y * ps_ref[:, cols]).astype(po_ref.dtype)


def _in_proj_prompt_kernel(x_ref, g_ref, w_ref, pw_ref, ps_ref,
                           k_ref, v_ref, qt_ref, kb_ref, vt_ref, po_ref, tail_ref, ext_ref, *, tm):
    t = pl.program_id(1)
    halo = 2 * SUBLANES

    @pl.when(t == 0)
    def _():
        ext_ref[0:halo, :] = jnp.zeros((halo, POOL_WIDTH), F32)

    h = _rms(x_ref[...], g_ref[...], EPS).astype(BF16)
    z = jnp.dot(h, w_ref[...], preferred_element_type=F32)
    u = z[:, :POOL_WIDTH]
    scale = DIFF_HD ** -0.5
    for hh in range(N_HEADS):
        q0 = POOL_WIDTH + hh * HEAD_W
        k0 = POOL_WIDTH + QK_WIDTH + hh * HEAD_W
        v0 = POOL_WIDTH + 2 * QK_WIDTH + hh * HEAD_W
        k_ref[pl.ds(hh, tm, stride=N_HEADS), :] = z[:, k0:k0 + HEAD_W]
        v_ref[pl.ds(hh, tm, stride=N_HEADS), :] = z[:, v0:v0 + HEAD_W]
        for s in range(tm // TQ):
            rows = slice(s * TQ, (s + 1) * TQ)
            qt_ref[hh, s] = (z[rows, q0:q0 + HEAD_W] * scale).T.astype(BF16)
            vt_ref[hh, s] = z[rows, v0:v0 + HEAD_W].T.astype(BF16)
            kb_ref[hh, s] = z[rows, k0:k0 + HEAD_W].astype(BF16)

    ext_ref[halo:halo + tm, :] = u
    pos = t * tm + lax.broadcasted_iota(jnp.int32, (tm, POOL_GROUP), 0)

    def shifted(j, cols):
        return ext_ref[halo - j:halo - j + tm, cols]

    def cnt(w):
        return jnp.minimum(pos + 1, w).astype(F32)

    _pool_groups(u, shifted, cnt, pw_ref, ps_ref, po_ref)
    tail_ref[...] = u[tm - halo:, :]
    ext_ref[0:halo, :] = u[tm - halo:, :]


def _in_proj_prompt(x, g, w_in, pool_w, pool_scale):
    B, T, D = x.shape
    tm = ROW_TILE
    nt = T // tm
    spt = tm // TQ
    row = lambda b, t: (b, t, 0)
    out_shape = (
        jax.ShapeDtypeStruct((B, T * N_HEADS, HEAD_W), F32),
        jax.ShapeDtypeStruct((B, T * N_HEADS, HEAD_W), F32),
        jax.ShapeDtypeStruct((B, N_HEADS, T // TQ, HEAD_W, TQ), BF16),
        jax.ShapeDtypeStruct((B, N_HEADS, T // TK, TK, HEAD_W), BF16),
        jax.ShapeDtypeStruct((B, N_HEADS, T // TK, HEAD_W, TK), BF16),
        jax.ShapeDtypeStruct((B, T, POOL_WIDTH), BF16),
        jax.ShapeDtypeStruct((B, 2 * SUBLANES, POOL_WIDTH), F32),
    )
    tile5 = lambda b, t: (b, 0, t, 0, 0)
    return pl.pallas_call(
        functools.partial(_in_proj_prompt_kernel, tm=tm),
        grid=(B, nt),
        in_specs=[
            pl.BlockSpec((None, tm, D), row),
            _const_spec((1, D)),
            _const_spec((D, IN_WIDTH)),
            _const_spec((len(POOL_WINDOWS), POOL_GROUP, POOL_GROUP)),
            _const_spec((1, POOL_WIDTH)),
        ],
        out_specs=(
            pl.BlockSpec((None, tm * N_HEADS, HEAD_W), row),
            pl.BlockSpec((None, tm * N_HEADS, HEAD_W), row),
            pl.BlockSpec((None, N_HEADS, spt, HEAD_W, TQ), tile5),
            pl.BlockSpec((None, N_HEADS, spt, TK, HEAD_W), tile5),
            pl.BlockSpec((None, N_HEADS, spt, HEAD_W, TK), tile5),
            pl.BlockSpec((None, tm, POOL_WIDTH), row),
            pl.BlockSpec((None, 2 * SUBLANES, POOL_WIDTH), lambda b, t: (b, 0, 0)),
        ),
        out_shape=out_shape,
        scratch_shapes=[pltpu.VMEM((tm + 2 * SUBLANES, POOL_WIDTH), F32)],
        compiler_params=_params(("arbitrary", "arbitrary")),
        name="in_proj_prompt",
    )(x, g, w_in, pool_w, pool_scale)


def _lambda_value(lamp_ref, lam0):
    lp = lamp_ref[...]
    s1 = jnp.sum(lp[0:1] * lp[1:2], axis=1, keepdims=True)
    s2 = jnp.sum(lp[2:3] * lp[3:4], axis=1, keepdims=True)
    return jnp.exp(s1) - jnp.exp(s2) + lam0


def _flash_kernel(lamp_ref, g_ref, qt_ref, k_ref, vt_ref, o_ref, rhs_ref, m_ref, acc_ref, p_ref, alpha_ref,
                  *, lam0):
    i = pl.program_id(1)
    two_tq = 2 * TQ
    slopes = [2.0 ** (-8.0 * (hh + 1) / N_HEADS) for hh in range(N_HEADS)]

    z64 = jnp.zeros((DIFF_HD, TQ), BF16)
    row = lax.broadcasted_iota(jnp.int32, (HEAD_W, two_tq), 0)
    col = lax.broadcasted_iota(jnp.int32, (HEAD_W, two_tq), 1)
    r_f = (col & (TQ - 1)).astype(F32)
    for hh in range(N_HEADS):
        qt = qt_ref[hh]
        top = jnp.concatenate([qt[:DIFF_HD], z64], axis=1)
        mid = jnp.concatenate([z64, qt[DIFF_HD:]], axis=1)
        bot = jnp.where(row == 0, -slopes[hh] * r_f, jnp.where(row == 1, slopes[hh], 0.0)).astype(BF16)
        rhs_ref[hh] = jnp.concatenate([top, mid, bot], axis=0)
    m_ref[...] = jnp.full(m_ref.shape, NEG_INF, F32)
    acc_ref[...] = jnp.zeros(acc_ref.shape, F32)
    p_ref[...] = jnp.zeros(p_ref.shape, BF16)
    alpha_ref[...] = jnp.ones(alpha_ref.shape, F32)

    krow = lax.broadcasted_iota(jnp.int32, (TK, HEAD_W), 0)
    klane = lax.broadcasted_iota(jnp.int32, (TK, HEAD_W), 1)
    aug_k = jnp.where(klane == 0, 1.0, jnp.where(klane == 1, krow.astype(F32), 0.0)).astype(BF16)
    ones_rows = jnp.ones((2 * SUBLANES, TK), BF16)

    def score_tiles(j, masked):
        scores = []
        for hh in range(N_HEADS):
            kk = jnp.concatenate([k_ref[hh, j], aug_k], axis=1)
            s = jnp.dot(kk, rhs_ref[hh], preferred_element_type=F32)
            if masked:
                c_idx = lax.broadcasted_iota(jnp.int32, (TK, two_tq), 0)
                r_idx = lax.broadcasted_iota(jnp.int32, (TK, two_tq), 1) & (TQ - 1)
                s = jnp.where(c_idx <= r_idx, s, NEG_INF)
            scores.append(s)
        return scores

    def softmax_tiles(scores, j):
        x = ((i - j) * TQ).astype(F32)
        probs = []
        for hh in range(N_HEADS):
            s = scores[hh]
            dlt = slopes[hh] * x
            m_old = m_ref[hh]
            m_new = jnp.maximum(m_old, jnp.max(s, axis=0, keepdims=True) - dlt)
            probs.append((jnp.exp(s - (m_new + dlt)).astype(BF16), jnp.exp(m_old - m_new)))
            m_ref[hh] = m_new
        return probs

    def accumulate(jv, probs):
        for hh in range(N_HEADS):
            p, alpha = probs[hh]
            v1 = jnp.concatenate([vt_ref[hh, jv], ones_rows], axis=0)
            pv = jnp.dot(v1, p, preferred_element_type=F32)
            acc_ref[hh] = alpha * acc_ref[hh] + pv

    def previous_probs():
        return [(p_ref[hh], alpha_ref[hh]) for hh in range(N_HEADS)]

    def body(j, carry):
        scores = score_tiles(j, False)
        accumulate(jnp.maximum(j - 1, 0), previous_probs())
        for hh, (p, alpha) in enumerate(softmax_tiles(scores, j)):
            p_ref[hh] = p
            alpha_ref[hh] = alpha
        return carry

    lax.fori_loop(0, i, body, 0)
    scores = score_tiles(i, True)
    accumulate(jnp.maximum(i - 1, 0), previous_probs())
    accumulate(i, softmax_tiles(scores, i))

    lam = _lambda_value(lamp_ref, lam0)
    for hh in range(N_HEADS):
        acc = acc_ref[hh]
        o1 = acc[:HEAD_W, :TQ] / acc[HEAD_W:HEAD_W + 1, :TQ]
        o2 = acc[:HEAD_W, TQ:] / acc[HEAD_W:HEAD_W + 1, TQ:]
        ot = o1 - lam * o2
        ms = jnp.mean(ot * ot, axis=0, keepdims=True)
        y = ot * lax.rsqrt(ms + SUBLN_EPS) * g_ref[...] * (1.0 - lam0)
        o_ref[:, hh * HEAD_W:(hh + 1) * HEAD_W] = y.T.astype(o_ref.dtype)


def _flash_diff_attn(qt, kb, vt, lamp, subln_g, lam0):
    B, H, nq = qt.shape[0], qt.shape[1], qt.shape[2]
    nk = kb.shape[2]
    T = nq * TQ
    whole = lambda b, i: (b, 0, 0, 0, 0)
    return pl.pallas_call(
        functools.partial(_flash_kernel, lam0=lam0),
        grid=(B, nq),
        in_specs=[
            _const_spec((4, DIFF_HD)),
            _const_spec((HEAD_W, 1)),
            pl.BlockSpec((None, H, None, HEAD_W, TQ), lambda b, i: (b, 0, i, 0, 0)),
            pl.BlockSpec((None, H, nk, TK, HEAD_W), whole, pipeline_mode=pl.Buffered(1)),
            pl.BlockSpec((None, H, nk, HEAD_W, TK), whole, pipeline_mode=pl.Buffered(1)),
        ],
        out_specs=pl.BlockSpec((None, TQ, H * HEAD_W), lambda b, i: (b, i, 0)),
        out_shape=jax.ShapeDtypeStruct((B, T, H * HEAD_W), BF16),
        scratch_shapes=[
            pltpu.VMEM((H, 2 * HEAD_W, 2 * TQ), BF16),
            pltpu.VMEM((H, 1, 2 * TQ), F32),
            pltpu.VMEM((H, HEAD_W + 2 * SUBLANES, 2 * TQ), F32),
            pltpu.VMEM((H, TK, 2 * TQ), BF16),
            pltpu.VMEM((H, 1, 2 * TQ), F32),
        ],
        compiler_params=_params(("arbitrary", "arbitrary")),
        name="flash_diff_attn",
    )(lamp, subln_g, qt, kb, vt)


def _out_q_kernel(x_ref, po_ref, o_ref, sg_ref, wout_ref, gc_ref, wq_ref, x1_ref, qc_ref, *, subln_scale):
    o = o_ref[...]
    if subln_scale is not None:
        parts = []
        for hh in range(N_HEADS):
            oh = o[:, hh * HEAD_W:(hh + 1) * HEAD_W]
            parts.append(_rms(oh, sg_ref[...], SUBLN_EPS) * subln_scale)
        o = jnp.concatenate(parts, axis=1)
    mix = jnp.dot(po_ref[...], wout_ref[0:POOL_WIDTH, :], preferred_element_type=F32)
    mix = mix + jnp.dot(o.astype(BF16), wout_ref[POOL_WIDTH:, :], preferred_element_type=F32)
    x1 = x_ref[...] + mix
    x1_ref[...] = x1
    hc = _rms(x1, gc_ref[...], EPS).astype(BF16)
    qc = jnp.dot(hc, wq_ref[...], preferred_element_type=F32) * (CROSS_HD ** -0.5)
    qc_ref[...] = qc.astype(BF16)


def _out_q(x, po, o, subln_g_row, w_out, g_cross, wq_c, tm, subln_scale=None):
    M, D = x.shape
    row = lambda t: (t, 0)
    return pl.pallas_call(
        functools.partial(_out_q_kernel, subln_scale=subln_scale),
        grid=(M // tm,),
        in_specs=[
            pl.BlockSpec((tm, D), row),
            pl.BlockSpec((tm, POOL_WIDTH), row),
            pl.BlockSpec((tm, N_HEADS * HEAD_W), row),
            _const_spec((1, HEAD_W)),
            _const_spec((D, D)),
            _const_spec((1, D)),
            _const_spec((D, D)),
        ],
        out_specs=(pl.BlockSpec((tm, D), row), pl.BlockSpec((tm, D), row)),
        out_shape=(jax.ShapeDtypeStruct((M, D), F32), jax.ShapeDtypeStruct((M, D), BF16)),
        compiler_params=_params(("arbitrary",)),
        name="out_q",
    )(x, po, o, subln_g_row, w_out, g_cross, wq_c)


def _mem_kv_kernel(mem_ref, wk_ref, wv_ref, mk_ref, mv_ref, mkt_ref, mvb_ref):
    m = mem_ref[...].astype(BF16)
    k = jnp.dot(m, wk_ref[...], preferred_element_type=F32)
    v = jnp.dot(m, wv_ref[...], preferred_element_type=F32)
    mk_ref[...] = k
    mv_ref[...] = v
    mkt_ref[...] = k.T.astype(BF16)
    mvb_ref[...] = v.astype(BF16)


def _mem_kv(mem, wk, wv):
    B, N, D = mem.shape
    blk = lambda b: (b, 0, 0)
    return pl.pallas_call(
        _mem_kv_kernel,
        grid=(B,),
        in_specs=[pl.BlockSpec((None, N, D), blk), _const_spec((D, D)), _const_spec((D, D))],
        out_specs=(pl.BlockSpec((None, N, D), blk), pl.BlockSpec((None, N, D), blk),
                   pl.BlockSpec((None, D, N), blk), pl.BlockSpec((None, N, D), blk)),
        out_shape=(jax.ShapeDtypeStruct((B, N, D), F32), jax.ShapeDtypeStruct((B, N, D), F32),
                   jax.ShapeDtypeStruct((B, D, N), BF16), jax.ShapeDtypeStruct((B, N, D), BF16)),
        compiler_params=_params(("arbitrary",)),
        name="mem_kv",
    )(mem, wk, wv)


def _cross_prompt_kernel(q_ref, kt_ref, v_ref, o_ref):
    for hh in range(N_CROSS_HEADS):
        cols = slice(hh * CROSS_HD, (hh + 1) * CROSS_HD)
        s = jnp.dot(q_ref[:, cols], kt_ref[cols, :], preferred_element_type=F32)
        m = jnp.max(s, axis=1, keepdims=True)
        p = jnp.exp(s - m)
        l = jnp.sum(p, axis=1, keepdims=True)
        o = jnp.dot(p.astype(BF16), v_ref[:, cols], preferred_element_type=F32) / l
        o_ref[:, cols] = o.astype(o_ref.dtype)


def _cross_prompt(qc, mkt, mvb, tm):
    B, T, D = qc.shape
    row = lambda b, t: (b, t, 0)
    blk = lambda b, t: (b, 0, 0)
    return pl.pallas_call(
        _cross_prompt_kernel,
        grid=(B, T // tm),
        in_specs=[pl.BlockSpec((None, tm, D), row),
                  pl.BlockSpec((None, D, N_MEM), blk),
                  pl.BlockSpec((None, N_MEM, D), blk)],
        out_specs=pl.BlockSpec((None, tm, D), row),
        out_shape=jax.ShapeDtypeStruct((B, T, D), BF16),
        compiler_params=_params(("arbitrary", "arbitrary")),
        name="cross_prompt",
    )(qc, mkt, mvb)


def _ffn_kernel(*refs, seq_mode, final_norm, tm):
    if seq_mode:
        (x1_ref, oc_ref, wo_ref, g_ref, wup_ref, cw_ref, cb_ref, wdn_ref, gfin_ref,
         out_ref, up_out_ref, x2_ref, h_ref, act_ref, halo_ref) = refs
    else:
        (x1_ref, oc_ref, p2_ref, p1_ref, wo_ref, g_ref, wup_ref, cw_ref, cb_ref, wdn_ref, gfin_ref,
         out_ref, up_out_ref, x2_ref, h_ref, act_ref) = refs
    t = pl.program_id(1)

    if seq_mode:
        @pl.when(t == 0)
        def _():
            halo_ref[...] = jnp.zeros(halo_ref.shape, F32)

    x2 = x1_ref[...] + jnp.dot(oc_ref[...], wo_ref[...], preferred_element_type=F32)
    x2_ref[...] = x2
    h_ref[...] = _rms(x2, g_ref[...], EPS).astype(BF16)

    def up_proj(c0):
        cols_a = slice(c0, c0 + FF_CHUNK)
        cols_g = slice(D_FF + c0, D_FF + c0 + FF_CHUNK)
        return (jnp.dot(h_ref[...], wup_ref[:, cols_a], preferred_element_type=F32),
                jnp.dot(h_ref[...], wup_ref[:, cols_g], preferred_element_type=F32))

    def conv_part(up, col0):
        cols = slice(col0, col0 + FF_CHUNK)
        if seq_mode:
            ext = jnp.concatenate([halo_ref[:, cols], up], axis=0)
            um1 = ext[SUBLANES - 1:SUBLANES - 1 + tm]
            um2 = ext[SUBLANES - 2:SUBLANES - 2 + tm]
            halo_ref[:, cols] = up[tm - SUBLANES:, :]
            up_out_ref[:, cols] = up[tm - SUBLANES:, :]
        else:
            um1 = p1_ref[:, cols]
            um2 = p2_ref[:, cols]
            up_out_ref[:, cols] = up
        c = cb_ref[:, cols] + um2 * cw_ref[0:1, cols]
        c = c + um1 * cw_ref[1:2, cols]
        return c + up * cw_ref[2:3, cols]

    starts = list(range(0, D_FF, FF_CHUNK))
    nxt = up_proj(starts[0])
    for n, c0 in enumerate(starts):
        up_a, up_g = nxt
        if n + 1 < len(starts):
            nxt = up_proj(starts[n + 1])
        a = conv_part(up_a, c0)
        g = conv_part(up_g, D_FF + c0)
        act_ref[:, c0:c0 + FF_CHUNK] = (g * (1.0 / (1.0 + jnp.exp(-g))) * a).astype(BF16)

    x3 = x2_ref[...] + jnp.dot(act_ref[...], wdn_ref[...], preferred_element_type=F32)
    if final_norm:
        x3 = _rms(x3, gfin_ref[...], EPS)
    out_ref[...] = x3


def _ffn(x1, oc, prev, wo, g_ffn, w_up, conv_w, conv_b, w_down, g_final, tm, final_norm):
    B, T, D = x1.shape
    seq_mode = prev is None
    W2 = 2 * D_FF
    row = lambda b, t: (b, t, 0)
    in_specs = [pl.BlockSpec((None, tm, D), row), pl.BlockSpec((None, tm, D), row)]
    args = [x1, oc]
    if not seq_mode:
        in_specs += [pl.BlockSpec((None, tm, W2), row), pl.BlockSpec((None, tm, W2), row)]
        args += list(prev)
    in_specs += [_const_spec((D, D)), _const_spec((1, D)), _const_spec((D, W2)),
                 _const_spec((CONV_W, W2)), _const_spec((1, W2)), _const_spec((D_FF, D)),
                 _const_spec((1, D))]
    args += [wo, g_ffn, w_up, conv_w, conv_b, w_down, g_final]
    scratch = [pltpu.VMEM((tm, D), F32), pltpu.VMEM((tm, D), BF16), pltpu.VMEM((tm, D_FF), BF16)]
    if seq_mode:
        up_rows = SUBLANES
        up_spec = pl.BlockSpec((None, SUBLANES, W2), lambda b, t: (b, 0, 0))
        scratch += [pltpu.VMEM((SUBLANES, W2), F32)]
    else:
        up_rows = T
        up_spec = pl.BlockSpec((None, tm, W2), row)
    return pl.pallas_call(
        functools.partial(_ffn_kernel, seq_mode=seq_mode, final_norm=final_norm, tm=tm),
        grid=(B, T // tm),
        in_specs=in_specs,
        out_specs=(pl.BlockSpec((None, tm, D), row), up_spec),
        out_shape=(jax.ShapeDtypeStruct((B, T, D), F32), jax.ShapeDtypeStruct((B, up_rows, W2), F32)),
        scratch_shapes=scratch,
        compiler_params=_params(("arbitrary", "arbitrary")),
        name="wo_ffn_seq" if seq_mode else "wo_ffn_step",
    )(*args)


def _in_proj_sample_kernel(x_ref, g_ref, w_ref, st_ref, pw_ref, ps_ref, z_ref, po_ref, nst_ref):
    h = _rms(x_ref[...], g_ref[...], EPS).astype(BF16)
    z = jnp.dot(h, w_ref[...], preferred_element_type=F32)
    z_ref[...] = z
    u = z[:, :POOL_WIDTH]

    def shifted(j, cols):
        return st_ref[POOL_PAST - j, :, cols]

    _pool_groups(u, shifted, lambda w: float(w), pw_ref, ps_ref, po_ref)
    for r in range(POOL_PAST - 1):
        nst_ref[r] = st_ref[r + 1]
    nst_ref[POOL_PAST - 1] = u


def _in_proj_sample(x, g, w_in, st_t, pool_w, pool_scale):
    M, D = x.shape
    return pl.pallas_call(
        _in_proj_sample_kernel,
        grid=(1,),
        in_specs=[_const_spec((M, D)), _const_spec((1, D)), _const_spec((D, IN_WIDTH)),
                  _const_spec((POOL_PAST, M, POOL_WIDTH)),
                  _const_spec((len(POOL_WINDOWS), POOL_GROUP, POOL_GROUP)),
                  _const_spec((1, POOL_WIDTH))],
        out_specs=(_const_spec((M, IN_WIDTH)), _const_spec((M, POOL_WIDTH)),
                   _const_spec((POOL_PAST, M, POOL_WIDTH))),
        out_shape=(jax.ShapeDtypeStruct((M, IN_WIDTH), F32),
                   jax.ShapeDtypeStruct((M, POOL_WIDTH), BF16),
                   jax.ShapeDtypeStruct((POOL_PAST, M, POOL_WIDTH), F32)),
        compiler_params=_params(("arbitrary",)),
        name="in_proj_sample",
    )(x, g, w_in, st_t, pool_w, pool_scale)


def _paged_attn_kernel(pt_ref, lamp_ref, q_ref, kn_ref, vn_ref, bias_ref, *rest, n_pages, lam0):
    del pt_ref
    k_refs = rest[:n_pages]
    v_refs = rest[n_pages:2 * n_pages]
    o_ref = rest[2 * n_pages]
    s_ref = rest[2 * n_pages + 1]
    rows = PAGE * N_HEADS
    nt_dims = (((1,), (1,)), ((), ()))

    q4 = q_ref[...] * (DIFF_HD ** -0.5)
    lane = lax.broadcasted_iota(jnp.int32, (N_HEADS, HEAD_W), 1)
    q_all = jnp.concatenate([jnp.where(lane < DIFF_HD, q4, 0.0),
                             jnp.where(lane >= DIFF_HD, q4, 0.0)], axis=0)
    q_all_b = q_all.astype(BF16)
    for p in range(n_pages):
        kp = k_refs[p][...].astype(BF16)
        s_ref[:, p * rows:(p + 1) * rows] = lax.dot_general(q_all_b, kp, nt_dims, preferred_element_type=F32)

    s = s_ref[...] + bias_ref[...]
    kn8 = jnp.concatenate([kn_ref[...], kn_ref[...]], axis=0)
    s_new = jnp.sum(q_all * kn8, axis=1, keepdims=True)
    m = jnp.maximum(jnp.max(s, axis=1, keepdims=True), s_new)
    pr = jnp.exp(s - m)
    pr_new = jnp.exp(s_new - m)
    l = jnp.sum(pr, axis=1, keepdims=True) + pr_new
    lam = _lambda_value(lamp_ref, lam0)
    rowi = lax.broadcasted_iota(jnp.int32, (2 * N_HEADS, 1), 0)
    coef = jnp.where(rowi < N_HEADS, 1.0, -lam) / l
    w = (pr * coef).astype(BF16)

    vn8 = jnp.concatenate([vn_ref[...], vn_ref[...]], axis=0)
    acc = (pr_new * coef) * vn8
    for p in range(n_pages):
        vp = v_refs[p][...].astype(BF16)
        acc = acc + jnp.dot(w[:, p * rows:(p + 1) * rows], vp, preferred_element_type=F32)
    o_ref[...] = acc[:N_HEADS] + acc[N_HEADS:]


def _paged_attn(zs, page_table, cache_k, cache_v, layer, lamp, lam0):
    Bs = zs.shape[0]
    n_pages = page_table.shape[1]
    n_past = n_pages * PAGE
    depth, n_phys = cache_k.shape[0], cache_k.shape[1]
    rows = PAGE * N_HEADS
    ck = cache_k.reshape(depth * n_phys, rows, HEAD_W)
    cv = cache_v.reshape(depth * n_phys, rows, HEAD_W)
    heads3 = lambda lo: zs[:, lo:lo + QK_WIDTH].reshape(Bs, N_HEADS, HEAD_W)
    q3 = heads3(POOL_WIDTH)
    kn3 = heads3(POOL_WIDTH + QK_WIDTH)
    vn3 = heads3(POOL_WIDTH + 2 * QK_WIDTH)

    slopes = 2.0 ** (-8.0 * np.arange(1, N_HEADS + 1) / N_HEADS)
    key_row = np.arange(n_past * N_HEADS)
    key_pos, key_head = key_row // N_HEADS, key_row % N_HEADS
    row_head = np.arange(2 * N_HEADS) % N_HEADS
    alibi = -slopes[row_head][:, None] * (n_past - key_pos)[None, :].astype(np.float64)
    bias = jnp.asarray(np.where(row_head[:, None] == key_head[None, :], alibi, -np.inf), F32)

    def page_spec(p):
        return pl.BlockSpec((None, rows, HEAD_W), lambda b, pt: (layer * n_phys + pt[b, p], 0, 0))

    const2 = lambda b, pt: (0, 0)
    row3 = pl.BlockSpec((None, N_HEADS, HEAD_W), lambda b, pt: (b, 0, 0))
    grid_spec = pltpu.PrefetchScalarGridSpec(
        num_scalar_prefetch=1,
        grid=(Bs,),
        in_specs=[pl.BlockSpec((4, DIFF_HD), const2), row3, row3, row3,
                  pl.BlockSpec((2 * N_HEADS, n_past * N_HEADS), const2)]
                 + [page_spec(p) for p in range(n_pages)]
                 + [page_spec(p) for p in range(n_pages)],
        out_specs=row3,
        scratch_shapes=[pltpu.VMEM((2 * N_HEADS, n_past * N_HEADS), F32)],
    )
    return pl.pallas_call(
        functools.partial(_paged_attn_kernel, n_pages=n_pages, lam0=lam0),
        grid_spec=grid_spec,
        out_shape=jax.ShapeDtypeStruct((Bs, N_HEADS, HEAD_W), F32),
        compiler_params=_params(("arbitrary",)),
        name="paged_diff_attn",
    )(page_table, lamp, q3, kn3, vn3, bias, *([ck] * n_pages), *([cv] * n_pages))


def _cross_sample_kernel(q_ref, expand_ref, k_ref, v_ref, o_ref):
    q = q_ref[...].astype(F32)
    rj = lax.broadcasted_iota(jnp.int32, (LANES, D_MODEL), 0)
    cj = lax.broadcasted_iota(jnp.int32, (LANES, D_MODEL), 1) >> 8
    q_blk = jnp.where(rj == cj, jnp.broadcast_to(q, (LANES, D_MODEL)), 0.0).astype(BF16)

    def mem_rows(ref):
        return jnp.concatenate([ref[:, hh, :] for hh in range(N_CROSS_HEADS)], axis=1)

    s = lax.dot_general(mem_rows(k_ref).astype(BF16), q_blk, (((1,), (1,)), ((), ())),
                        preferred_element_type=F32)
    m = jnp.max(s, axis=0, keepdims=True)
    p = jnp.exp(s - m)
    l = jnp.sum(p, axis=0, keepdims=True)
    lane = lax.broadcasted_iota(jnp.int32, (1, LANES), 1)
    pn = p * jnp.where(lane < N_CROSS_HEADS, 1.0 / l, 0.0)
    wb = jnp.dot(pn.astype(BF16), expand_ref[...], preferred_element_type=F32)
    prod = wb * mem_rows(v_ref)
    part = jnp.sum(prod.reshape(N_MEM // SUBLANES, SUBLANES, D_MODEL), axis=0)
    o_ref[...] = jnp.sum(part, axis=0, keepdims=True).astype(o_ref.dtype)


def _cross_sample(qc3, mem_k, mem_v, layer):
    Bs = qc3.shape[0]
    mem_blk = pl.BlockSpec((None, None, N_MEM, N_CROSS_HEADS, CROSS_HD), lambda b: (layer, b, 0, 0, 0))
    expand = np.zeros((LANES, D_MODEL), np.float32)
    for j in range(N_CROSS_HEADS):
        expand[j, j * CROSS_HD:(j + 1) * CROSS_HD] = 1.0
    expand = jnp.asarray(expand, BF16)
    blk = lambda b: (b, 0, 0)
    return pl.pallas_call(
        _cross_sample_kernel,
        grid=(Bs,),
        in_specs=[pl.BlockSpec((None, 1, D_MODEL), blk), _const_spec((LANES, D_MODEL)), mem_blk, mem_blk],
        out_specs=pl.BlockSpec((None, 1, D_MODEL), blk),
        out_shape=jax.ShapeDtypeStruct((Bs, 1, D_MODEL), BF16),
        compiler_params=_params(("arbitrary",)),
        name="cross_sample",
    )(qc3, expand, mem_k, mem_v)


def kernel(x_prompt, x_sample, mem_prompt, cache_k, cache_v, cache_mem_k, cache_mem_v, state_pool, state_conv, page_table, g_mix, w_in, pool_w, pool_scale, lam_q1, lam_k1, lam_q2, lam_k2, subln_g, w_out, g_cross, wq_c, wk_c, wv_c, wo_c, g_ffn, w_up, conv_w, conv_b, w_down, g_final):
    depth = w_in.shape[0]
    Bp, T, D = x_prompt.shape
    Bs = x_sample.shape[0]
    W2 = 2 * D_FF
    bf = lambda a: a.astype(BF16)
    w_in_b, pool_w_b, w_out_b = bf(w_in), bf(pool_w), bf(w_out)
    wq_b, wk_b, wv_b, wo_b, w_up_b, w_down_b = bf(wq_c), bf(wk_c), bf(wv_c), bf(wo_c), bf(w_up), bf(w_down)
    gfin = g_final.reshape(1, D)

    xp = x_prompt
    xs = x_sample.reshape(Bs, D)
    outs = {n: [] for n in ("kp", "vp", "mkp", "mvp", "pp", "cp", "ks", "vs", "ps", "cs")}
    for l in range(depth):
        lam0 = _lambda_init(l)
        last = l == depth - 1
        g_mix_l = g_mix[l].reshape(1, D)
        g_cross_l = g_cross[l].reshape(1, D)
        g_ffn_l = g_ffn[l].reshape(1, D)
        ps_l = pool_scale[l].reshape(1, POOL_WIDTH)
        cb_l = conv_b[l].reshape(1, W2)
        lamp = jnp.stack([lam_q1[l], lam_k1[l], lam_q2[l], lam_k2[l]])

        mk, mv, mkt, mvb = _mem_kv(mem_prompt, wk_b[l], wv_b[l])
        kp, vp, qt, kb, vt, po, tail = _in_proj_prompt(xp, g_mix_l, w_in_b[l], pool_w_b[l], ps_l)
        o = _flash_diff_attn(qt, kb, vt, lamp, subln_g[l].reshape(HEAD_W, 1), lam0)
        x1, qc = _out_q(xp.reshape(Bp * T, D), po.reshape(Bp * T, POOL_WIDTH), o.reshape(Bp * T, -1),
                        subln_g[l].reshape(1, HEAD_W), w_out_b[l], g_cross_l, wq_b[l], ROW_TILE)
        oc = _cross_prompt(qc.reshape(Bp, T, D), mkt, mvb, ROW_TILE)
        xp, up_tail = _ffn(x1.reshape(Bp, T, D), oc, None, wo_b[l], g_ffn_l, w_up_b[l], conv_w[l], cb_l,
                           w_down_b[l], gfin, ROW_TILE, last)
        outs["kp"].append(kp.reshape(Bp, T, N_HEADS, HEAD_W))
        outs["vp"].append(vp.reshape(Bp, T, N_HEADS, HEAD_W))
        outs["mkp"].append(mk.reshape(Bp, N_MEM, N_CROSS_HEADS, CROSS_HD))
        outs["mvp"].append(mv.reshape(Bp, N_MEM, N_CROSS_HEADS, CROSS_HD))
        outs["pp"].append(tail[:, 2 * SUBLANES - POOL_PAST:])
        outs["cp"].append(up_tail[:, SUBLANES - (CONV_W - 1):])

        st_t = jnp.swapaxes(state_pool[l], 0, 1)
        zs, pos, nst = _in_proj_sample(xs, g_mix_l, w_in_b[l], st_t, pool_w_b[l], ps_l)
        o_raw = _paged_attn(zs, page_table, cache_k, cache_v, l, lamp, lam0)
        x1s, qcs = _out_q(xs, pos, o_raw.reshape(Bs, -1), subln_g[l].reshape(1, HEAD_W), w_out_b[l],
                          g_cross_l, wq_b[l], Bs, subln_scale=1.0 - lam0)
        ocs = _cross_sample(qcs.reshape(Bs, 1, D), cache_mem_k, cache_mem_v, l)
        prev = (state_conv[l][:, 0].reshape(1, Bs, W2), state_conv[l][:, 1].reshape(1, Bs, W2))
        xs3, up_s = _ffn(x1s.reshape(1, Bs, D), ocs.reshape(1, Bs, D), prev, wo_b[l], g_ffn_l, w_up_b[l],
                         conv_w[l], cb_l, w_down_b[l], gfin, Bs, last)
        xs = xs3.reshape(Bs, D)
        outs["ks"].append(zs[:, POOL_WIDTH + QK_WIDTH:POOL_WIDTH + 2 * QK_WIDTH].reshape(Bs, 1, N_HEADS, HEAD_W))
        outs["vs"].append(zs[:, POOL_WIDTH + 2 * QK_WIDTH:].reshape(Bs, 1, N_HEADS, HEAD_W))
        outs["ps"].append(jnp.swapaxes(nst, 0, 1))
        outs["cs"].append(jnp.stack([state_conv[l][:, 1], up_s.reshape(Bs, W2)], axis=1))

    st = lambda n: jnp.stack(outs[n])
    return (xp, xs.reshape(Bs, 1, D), st("kp"), st("vp"), st("mkp"), st("mvp"), st("pp"), st("cp"),
            st("ks"), st("vs"), st("ps"), st("cs"))
```

```python
import functools
import math

import jax
import jax.numpy as jnp
import numpy as np
from jax import lax
from jax.experimental import pallas as pl
from jax.experimental.pallas import tpu as pltpu

F32 = jnp.float32
BF16 = jnp.bfloat16

D_MODEL = 1024
POOL_WIDTH = 512
POOL_WINDOWS = (2, 4, 8, 16)
POOL_GROUP = 128
POOL_PAST = 15
N_HEADS = 4
HEAD_W = 128
DIFF_HD = 64
QK_WIDTH = 512
IN_WIDTH = 2048
N_MEM = 256
N_CROSS_HEADS = 4
CROSS_HD = 256
D_FF = 2816
CONV_W = 3
EPS = 1e-6
SUBLN_EPS = 1e-5
PAGE = 128

LANES = 128
SUBLANES = 8
VMEM_LIMIT = 56 * 1024 * 1024

ROW_TILE = 512
TQ = 256
TK = 256
FF_CHUNK = 256
CROSS_ROWS_PER_STEP = 2
NEG_INF = float("-inf")


def _lambda_init(layer_idx):
    return 0.8 - 0.6 * math.exp(-0.3 * layer_idx)


def _rms(x, g, eps):
    ms = jnp.mean(x * x, axis=-1, keepdims=True)
    return x * lax.rsqrt(ms + eps) * g


def _params(sem, vmem=VMEM_LIMIT):
    return pltpu.CompilerParams(dimension_semantics=sem, vmem_limit_bytes=vmem)


def _const_spec(shape):
    nd = len(shape)
    return pl.BlockSpec(shape, lambda *_: (0,) * nd)


def _pool_groups(u, shifted, cnt_fn, pw_ref, ps_ref, po_ref):
    for g, w in enumerate(POOL_WINDOWS):
        cols = slice(g * POOL_GROUP, (g + 1) * POOL_GROUP)
        ug = u[:, cols]
        ws = ug
        for j in range(1, w):
            ws = ws + shifted(j, cols)
        p = ws / cnt_fn(w) - ug
        y = jnp.dot(p.astype(BF16), pw_ref[g], preferred_element_type=F32)
        po_ref[:, cols] = (y * ps_ref[:, cols]).astype(po_ref.dtype)


def _in_proj_prompt_kernel(x_ref, g_ref, w_ref, pw_ref, ps_ref,
                           k_ref, v_ref, qt_ref, kb_ref, vt_ref, po_ref, tail_ref, ext_ref, *, tm):
    t = pl.program_id(1)
    halo = 2 * SUBLANES

    @pl.when(t == 0)
    def _():
        ext_ref[0:halo, :] = jnp.zeros((halo, POOL_WIDTH), F32)

    h = _rms(x_ref[...], g_ref[...], EPS).astype(BF16)
    z = jnp.dot(h, w_ref[...], preferred_element_type=F32)
    u = z[:, :POOL_WIDTH]
    scale = DIFF_HD ** -0.5 * LOG2E
    for hh in range(N_HEADS):
        q0 = POOL_WIDTH + hh * HEAD_W
        k0 = POOL_WIDTH + QK_WIDTH + hh * HEAD_W
        v0 = POOL_WIDTH + 2 * QK_WIDTH + hh * HEAD_W
        k_ref[pl.ds(hh, tm, stride=N_HEADS), :] = z[:, k0:k0 + HEAD_W]
        v_ref[pl.ds(hh, tm, stride=N_HEADS), :] = z[:, v0:v0 + HEAD_W]
        for s in range(tm // TQ):
            rows = slice(s * TQ, (s + 1) * TQ)
            qt_ref[hh, s] = (z[rows, q0:q0 + HEAD_W] * scale).T.astype(BF16)
            vt_ref[hh, s] = z[rows, v0:v0 + HEAD_W].T.astype(BF16)
            kb_ref[hh, s] = z[rows, k0:k0 + HEAD_W].astype(BF16)

    ext_ref[halo:halo + tm, :] = u
    pos = t * tm + lax.broadcasted_iota(jnp.int32, (tm, POOL_GROUP), 0)

    def shifted(j, cols):
        return ext_ref[halo - j:halo - j + tm, cols]

    def cnt(w):
        return jnp.minimum(pos + 1, w).astype(F32)

    _pool_groups(u, shifted, cnt, pw_ref, ps_ref, po_ref)
    tail_ref[...] = u[tm - halo:, :]
    ext_ref[0:halo, :] = u[tm - halo:, :]


def _in_proj_prompt(x, g, w_in, pool_w, pool_scale):
    B, T, D = x.shape
    tm = ROW_TILE
    nt = T // tm
    spt = tm // TQ
    row = lambda b, t: (b, t, 0)
    out_shape = (
        jax.ShapeDtypeStruct((B, T * N_HEADS, HEAD_W), F32),
        jax.ShapeDtypeStruct((B, T * N_HEADS, HEAD_W), F32),
        jax.ShapeDtypeStruct((B, N_HEADS, T // TQ, HEAD_W, TQ), BF16),
        jax.ShapeDtypeStruct((B, N_HEADS, T // TK, TK, HEAD_W), BF16),
        jax.ShapeDtypeStruct((B, N_HEADS, T // TK, HEAD_W, TK), BF16),
        jax.ShapeDtypeStruct((B, T, POOL_WIDTH), BF16),
        jax.ShapeDtypeStruct((B, 2 * SUBLANES, POOL_WIDTH), F32),
    )
    tile5 = lambda b, t: (b, 0, t, 0, 0)
    return pl.pallas_call(
        functools.partial(_in_proj_prompt_kernel, tm=tm),
        grid=(B, nt),
        in_specs=[
            pl.BlockSpec((None, tm, D), row),
            _const_spec((1, D)),
            _const_spec((D, IN_WIDTH)),
            _const_spec((len(POOL_WINDOWS), POOL_GROUP, POOL_GROUP)),
            _const_spec((1, POOL_WIDTH)),
        ],
        out_specs=(
            pl.BlockSpec((None, tm * N_HEADS, HEAD_W), row),
            pl.BlockSpec((None, tm * N_HEADS, HEAD_W), row),
            pl.BlockSpec((None, N_HEADS, spt, HEAD_W, TQ), tile5),
            pl.BlockSpec((None, N_HEADS, spt, TK, HEAD_W), tile5),
            pl.BlockSpec((None, N_HEADS, spt, HEAD_W, TK), tile5),
            pl.BlockSpec((None, tm, POOL_WIDTH), row),
            pl.BlockSpec((None, 2 * SUBLANES, POOL_WIDTH), lambda b, t: (b, 0, 0)),
        ),
        out_shape=out_shape,
        scratch_shapes=[pltpu.VMEM((tm + 2 * SUBLANES, POOL_WIDTH), F32)],
        compiler_params=_params(("arbitrary", "arbitrary")),
        name="in_proj_prompt",
    )(x, g, w_in, pool_w, pool_scale)


def _lambda_value(lamp_ref, lam0):
    lp = lamp_ref[...]
    s1 = jnp.sum(lp[0:1] * lp[1:2], axis=1, keepdims=True)
    s2 = jnp.sum(lp[2:3] * lp[3:4], axis=1, keepdims=True)
    return jnp.exp(s1) - jnp.exp(s2) + lam0


def _bf16_terms(x, n):
    terms = []
    for _ in range(n):
        t = float(np.asarray(x, np.float32).astype(jnp.bfloat16).astype(np.float32))
        terms.append(t)
        x -= t
    return terms


LOG2E = math.log2(math.e)
LOG2E_TERMS = _bf16_terms(LOG2E, 3)


def _flash_kernel(lamp_ref, g_ref, qt_ref, k_ref, vt_ref, o_ref, rhs_ref, m_ref, acc_ref, p_ref, alpha_ref,
                  *, lam0):
    i = pl.program_id(1)
    two_tq = 2 * TQ
    n_terms = len(LOG2E_TERMS)
    slopes = [2.0 ** (-8.0 * (hh + 1) / N_HEADS) for hh in range(N_HEADS)]
    log2e_hi = float(np.float32(LOG2E))
    log2e_lo = LOG2E - log2e_hi

    z64 = jnp.zeros((DIFF_HD, TQ), BF16)
    row = lax.broadcasted_iota(jnp.int32, (HEAD_W, two_tq), 0)
    col = lax.broadcasted_iota(jnp.int32, (HEAD_W, two_tq), 1)
    r_f = (col & (TQ - 1)).astype(F32)
    for hh in range(N_HEADS):
        qt = qt_ref[hh]
        top = jnp.concatenate([qt[:DIFF_HD], z64], axis=1)
        mid = jnp.concatenate([z64, qt[DIFF_HD:]], axis=1)
        bot = jnp.where((row >= n_terms) & (row < 2 * n_terms), slopes[hh] * r_f, 0.0)
        for t, e_t in enumerate(LOG2E_TERMS):
            bot = jnp.where(row == t, slopes[hh] * e_t, bot)
        rhs_ref[hh] = jnp.concatenate([top, mid, bot.astype(BF16)], axis=0)
    m_ref[...] = jnp.full(m_ref.shape, NEG_INF, F32)
    acc_ref[...] = jnp.zeros(acc_ref.shape, F32)
    p_ref[...] = jnp.zeros(p_ref.shape, BF16)
    alpha_ref[...] = jnp.ones(alpha_ref.shape, F32)

    krow = lax.broadcasted_iota(jnp.int32, (TK, HEAD_W), 0)
    klane = lax.broadcasted_iota(jnp.int32, (TK, HEAD_W), 1)
    aug_k = jnp.where(klane < n_terms, krow.astype(F32), 0.0)
    for t, e_t in enumerate(LOG2E_TERMS):
        aug_k = jnp.where(klane == n_terms + t, -e_t, aug_k)
    aug_k = aug_k.astype(BF16)
    ones_rows = jnp.ones((2 * SUBLANES, TK), BF16)

    def score_tiles(j, masked):
        scores = []
        for hh in range(N_HEADS):
            kk = jnp.concatenate([k_ref[hh, j], aug_k], axis=1)
            s = jnp.dot(kk, rhs_ref[hh], preferred_element_type=F32)
            if masked:
                c_idx = lax.broadcasted_iota(jnp.int32, (TK, two_tq), 0)
                r_idx = lax.broadcasted_iota(jnp.int32, (TK, two_tq), 1) & (TQ - 1)
                s = jnp.where(c_idx <= r_idx, s, NEG_INF)
            scores.append(s)
        return scores

    def softmax_tiles(scores, j):
        x = ((i - j) * TQ).astype(F32)
        probs = []
        for hh in range(N_HEADS):
            s = scores[hh]
            sx = slopes[hh] * x
            dlt = sx * log2e_hi + sx * log2e_lo
            m_old = m_ref[hh]
            m_new = jnp.maximum(m_old, jnp.max(s, axis=0, keepdims=True) - dlt)
            probs.append((jnp.exp2(s - (m_new + dlt)).astype(BF16), jnp.exp2(m_old - m_new)))
            m_ref[hh] = m_new
        return probs

    def accumulate(jv, probs):
        for hh in range(N_HEADS):
            p, alpha = probs[hh]
            v1 = jnp.concatenate([vt_ref[hh, jv], ones_rows], axis=0)
            pv = jnp.dot(v1, p, preferred_element_type=F32)
            acc_ref[hh] = alpha * acc_ref[hh] + pv

    def previous_probs():
        return [(p_ref[hh], alpha_ref[hh]) for hh in range(N_HEADS)]

    def body(j, carry):
        scores = score_tiles(j, False)
        accumulate(jnp.maximum(j - 1, 0), previous_probs())
        for hh, (p, alpha) in enumerate(softmax_tiles(scores, j)):
            p_ref[hh] = p
            alpha_ref[hh] = alpha
        return carry

    lax.fori_loop(0, i, body, 0)
    scores = score_tiles(i, True)
    accumulate(jnp.maximum(i - 1, 0), previous_probs())
    accumulate(i, softmax_tiles(scores, i))

    lam = _lambda_value(lamp_ref, lam0)
    for hh in range(N_HEADS):
        acc = acc_ref[hh]
        o1 = acc[:HEAD_W, :TQ] / acc[HEAD_W:HEAD_W + 1, :TQ]
        o2 = acc[:HEAD_W, TQ:] / acc[HEAD_W:HEAD_W + 1, TQ:]
        ot = o1 - lam * o2
        ms = jnp.mean(ot * ot, axis=0, keepdims=True)
        y = ot * lax.rsqrt(ms + SUBLN_EPS) * g_ref[...] * (1.0 - lam0)
        o_ref[:, hh * HEAD_W:(hh + 1) * HEAD_W] = y.T.astype(o_ref.dtype)


def _flash_diff_attn(qt, kb, vt, lamp, subln_g, lam0):
    B, H, nq = qt.shape[0], qt.shape[1], qt.shape[2]
    nk = kb.shape[2]
    T = nq * TQ
    whole = lambda b, i: (b, 0, 0, 0, 0)
    return pl.pallas_call(
        functools.partial(_flash_kernel, lam0=lam0),
        grid=(B, nq),
        in_specs=[
            _const_spec((4, DIFF_HD)),
            _const_spec((HEAD_W, 1)),
            pl.BlockSpec((None, H, None, HEAD_W, TQ), lambda b, i: (b, 0, i, 0, 0)),
            pl.BlockSpec((None, H, nk, TK, HEAD_W), whole, pipeline_mode=pl.Buffered(1)),
            pl.BlockSpec((None, H, nk, HEAD_W, TK), whole, pipeline_mode=pl.Buffered(1)),
        ],
        out_specs=pl.BlockSpec((None, TQ, H * HEAD_W), lambda b, i: (b, i, 0)),
        out_shape=jax.ShapeDtypeStruct((B, T, H * HEAD_W), BF16),
        scratch_shapes=[
            pltpu.VMEM((H, 2 * HEAD_W, 2 * TQ), BF16),
            pltpu.VMEM((H, 1, 2 * TQ), F32),
            pltpu.VMEM((H, HEAD_W + 2 * SUBLANES, 2 * TQ), F32),
            pltpu.VMEM((H, TK, 2 * TQ), BF16),
            pltpu.VMEM((H, 1, 2 * TQ), F32),
        ],
        compiler_params=_params(("arbitrary", "arbitrary")),
        name="flash_diff_attn",
    )(lamp, subln_g, qt, kb, vt)


def _out_q_kernel(x_ref, po_ref, o_ref, sg_ref, wout_ref, gc_ref, wq_ref, x1_ref, qc_ref, *, subln_scale):
    o = o_ref[...]
    if subln_scale is not None:
        parts = []
        for hh in range(N_HEADS):
            oh = o[:, hh * HEAD_W:(hh + 1) * HEAD_W]
            parts.append(_rms(oh, sg_ref[...], SUBLN_EPS) * subln_scale)
        o = jnp.concatenate(parts, axis=1)
    mix = jnp.dot(po_ref[...], wout_ref[0:POOL_WIDTH, :], preferred_element_type=F32)
    mix = mix + jnp.dot(o.astype(BF16), wout_ref[POOL_WIDTH:, :], preferred_element_type=F32)
    x1 = x_ref[...] + mix
    x1_ref[...] = x1
    hc = _rms(x1, gc_ref[...], EPS).astype(BF16)
    qc = jnp.dot(hc, wq_ref[...], preferred_element_type=F32) * (CROSS_HD ** -0.5)
    qc_ref[...] = qc.astype(BF16)


def _out_q(x, po, o, subln_g_row, w_out, g_cross, wq_c, tm, subln_scale=None):
    M, D = x.shape
    row = lambda t: (t, 0)
    return pl.pallas_call(
        functools.partial(_out_q_kernel, subln_scale=subln_scale),
        grid=(M // tm,),
        in_specs=[
            pl.BlockSpec((tm, D), row),
            pl.BlockSpec((tm, POOL_WIDTH), row),
            pl.BlockSpec((tm, N_HEADS * HEAD_W), row),
            _const_spec((1, HEAD_W)),
            _const_spec((D, D)),
            _const_spec((1, D)),
            _const_spec((D, D)),
        ],
        out_specs=(pl.BlockSpec((tm, D), row), pl.BlockSpec((tm, D), row)),
        out_shape=(jax.ShapeDtypeStruct((M, D), F32), jax.ShapeDtypeStruct((M, D), BF16)),
        compiler_params=_params(("arbitrary",)),
        name="out_q",
    )(x, po, o, subln_g_row, w_out, g_cross, wq_c)


def _mem_kv_kernel(mem_ref, wk_ref, wv_ref, mk_ref, mv_ref, mkt_ref, mvb_ref):
    m = mem_ref[...].astype(BF16)
    k = jnp.dot(m, wk_ref[...], preferred_element_type=F32)
    v = jnp.dot(m, wv_ref[...], preferred_element_type=F32)
    mk_ref[...] = k
    mv_ref[...] = v
    mkt_ref[...] = k.T.astype(BF16)
    mvb_ref[...] = v.astype(BF16)


def _mem_kv(mem, wk, wv):
    B, N, D = mem.shape
    blk = lambda b: (b, 0, 0)
    return pl.pallas_call(
        _mem_kv_kernel,
        grid=(B,),
        in_specs=[pl.BlockSpec((None, N, D), blk), _const_spec((D, D)), _const_spec((D, D))],
        out_specs=(pl.BlockSpec((None, N, D), blk), pl.BlockSpec((None, N, D), blk),
                   pl.BlockSpec((None, D, N), blk), pl.BlockSpec((None, N, D), blk)),
        out_shape=(jax.ShapeDtypeStruct((B, N, D), F32), jax.ShapeDtypeStruct((B, N, D), F32),
                   jax.ShapeDtypeStruct((B, D, N), BF16), jax.ShapeDtypeStruct((B, N, D), BF16)),
        compiler_params=_params(("arbitrary",)),
        name="mem_kv",
    )(mem, wk, wv)


def _cross_prompt_kernel(q_ref, kt_ref, v_ref, o_ref):
    for hh in range(N_CROSS_HEADS):
        cols = slice(hh * CROSS_HD, (hh + 1) * CROSS_HD)
        s = jnp.dot(q_ref[:, cols], kt_ref[cols, :], preferred_element_type=F32)
        m = jnp.max(s, axis=1, keepdims=True)
        p = jnp.exp(s - m)
        l = jnp.sum(p, axis=1, keepdims=True)
        o = jnp.dot(p.astype(BF16), v_ref[:, cols], preferred_element_type=F32) / l
        o_ref[:, cols] = o.astype(o_ref.dtype)


def _cross_prompt(qc, mkt, mvb, tm):
    B, T, D = qc.shape
    row = lambda b, t: (b, t, 0)
    blk = lambda b, t: (b, 0, 0)
    return pl.pallas_call(
        _cross_prompt_kernel,
        grid=(B, T // tm),
        in_specs=[pl.BlockSpec((None, tm, D), row),
                  pl.BlockSpec((None, D, N_MEM), blk),
                  pl.BlockSpec((None, N_MEM, D), blk)],
        out_specs=pl.BlockSpec((None, tm, D), row),
        out_shape=jax.ShapeDtypeStruct((B, T, D), BF16),
        compiler_params=_params(("arbitrary", "arbitrary")),
        name="cross_prompt",
    )(qc, mkt, mvb)


def _ffn_kernel(*refs, seq_mode, final_norm, tm):
    if seq_mode:
        (x1_ref, oc_ref, wo_ref, g_ref, wup_ref, cw_ref, cb_ref, wdn_ref, gfin_ref,
         out_ref, up_out_ref, x2_ref, h_ref, act_ref, halo_ref) = refs
    else:
        (x1_ref, oc_ref, p2_ref, p1_ref, wo_ref, g_ref, wup_ref, cw_ref, cb_ref, wdn_ref, gfin_ref,
         out_ref, up_out_ref, x2_ref, h_ref, act_ref) = refs
    t = pl.program_id(1)

    if seq_mode:
        @pl.when(t == 0)
        def _():
            halo_ref[...] = jnp.zeros(halo_ref.shape, F32)

    x2 = x1_ref[...] + jnp.dot(oc_ref[...], wo_ref[...], preferred_element_type=F32)
    x2_ref[...] = x2
    h_ref[...] = _rms(x2, g_ref[...], EPS).astype(BF16)

    def up_proj(c0):
        cols_a = slice(c0, c0 + FF_CHUNK)
        cols_g = slice(D_FF + c0, D_FF + c0 + FF_CHUNK)
        return (jnp.dot(h_ref[...], wup_ref[:, cols_a], preferred_element_type=F32),
                jnp.dot(h_ref[...], wup_ref[:, cols_g], preferred_element_type=F32))

    def conv_part(up, col0):
        cols = slice(col0, col0 + FF_CHUNK)
        if seq_mode:
            ext = jnp.concatenate([halo_ref[:, cols], up], axis=0)
            um1 = ext[SUBLANES - 1:SUBLANES - 1 + tm]
            um2 = ext[SUBLANES - 2:SUBLANES - 2 + tm]
            halo_ref[:, cols] = up[tm - SUBLANES:, :]
            up_out_ref[:, cols] = up[tm - SUBLANES:, :]
        else:
            um1 = p1_ref[:, cols]
            um2 = p2_ref[:, cols]
            up_out_ref[:, cols] = up
        c = cb_ref[:, cols] + um2 * cw_ref[0:1, cols]
        c = c + um1 * cw_ref[1:2, cols]
        return c + up * cw_ref[2:3, cols]

    starts = list(range(0, D_FF, FF_CHUNK))
    nxt = up_proj(starts[0])
    for n, c0 in enumerate(starts):
        up_a, up_g = nxt
        if n + 1 < len(starts):
            nxt = up_proj(starts[n + 1])
        a = conv_part(up_a, c0)
        g = conv_part(up_g, D_FF + c0)
        act_ref[:, c0:c0 + FF_CHUNK] = (g * (1.0 / (1.0 + jnp.exp(-g))) * a).astype(BF16)

    x3 = x2_ref[...] + jnp.dot(act_ref[...], wdn_ref[...], preferred_element_type=F32)
    if final_norm:
        x3 = _rms(x3, gfin_ref[...], EPS)
    out_ref[...] = x3


def _ffn(x1, oc, prev, wo, g_ffn, w_up, conv_w, conv_b, w_down, g_final, tm, final_norm):
    B, T, D = x1.shape
    seq_mode = prev is None
    W2 = 2 * D_FF
    row = lambda b, t: (b, t, 0)
    in_specs = [pl.BlockSpec((None, tm, D), row), pl.BlockSpec((None, tm, D), row)]
    args = [x1, oc]
    if not seq_mode:
        in_specs += [pl.BlockSpec((None, tm, W2), row), pl.BlockSpec((None, tm, W2), row)]
        args += list(prev)
    in_specs += [_const_spec((D, D)), _const_spec((1, D)), _const_spec((D, W2)),
                 _const_spec((CONV_W, W2)), _const_spec((1, W2)), _const_spec((D_FF, D)),
                 _const_spec((1, D))]
    args += [wo, g_ffn, w_up, conv_w, conv_b, w_down, g_final]
    scratch = [pltpu.VMEM((tm, D), F32), pltpu.VMEM((tm, D), BF16), pltpu.VMEM((tm, D_FF), BF16)]
    if seq_mode:
        up_rows = SUBLANES
        up_spec = pl.BlockSpec((None, SUBLANES, W2), lambda b, t: (b, 0, 0))
        scratch += [pltpu.VMEM((SUBLANES, W2), F32)]
    else:
        up_rows = T
        up_spec = pl.BlockSpec((None, tm, W2), row)
    return pl.pallas_call(
        functools.partial(_ffn_kernel, seq_mode=seq_mode, final_norm=final_norm, tm=tm),
        grid=(B, T // tm),
        in_specs=in_specs,
        out_specs=(pl.BlockSpec((None, tm, D), row), up_spec),
        out_shape=(jax.ShapeDtypeStruct((B, T, D), F32), jax.ShapeDtypeStruct((B, up_rows, W2), F32)),
        scratch_shapes=scratch,
        compiler_params=_params(("arbitrary", "arbitrary")),
        name="wo_ffn_seq" if seq_mode else "wo_ffn_step",
    )(*args)


def _in_proj_sample_kernel(x_ref, g_ref, w_ref, st_ref, pw_ref, ps_ref, z_ref, po_ref, nst_ref):
    h = _rms(x_ref[...], g_ref[...], EPS).astype(BF16)
    z = jnp.dot(h, w_ref[...], preferred_element_type=F32)
    z_ref[...] = z
    u = z[:, :POOL_WIDTH]

    def shifted(j, cols):
        return st_ref[POOL_PAST - j, :, cols]

    _pool_groups(u, shifted, lambda w: float(w), pw_ref, ps_ref, po_ref)
    for r in range(POOL_PAST - 1):
        nst_ref[r] = st_ref[r + 1]
    nst_ref[POOL_PAST - 1] = u


def _in_proj_sample(x, g, w_in, st_t, pool_w, pool_scale):
    M, D = x.shape
    return pl.pallas_call(
        _in_proj_sample_kernel,
        grid=(1,),
        in_specs=[_const_spec((M, D)), _const_spec((1, D)), _const_spec((D, IN_WIDTH)),
                  _const_spec((POOL_PAST, M, POOL_WIDTH)),
                  _const_spec((len(POOL_WINDOWS), POOL_GROUP, POOL_GROUP)),
                  _const_spec((1, POOL_WIDTH))],
        out_specs=(_const_spec((M, IN_WIDTH)), _const_spec((M, POOL_WIDTH)),
                   _const_spec((POOL_PAST, M, POOL_WIDTH))),
        out_shape=(jax.ShapeDtypeStruct((M, IN_WIDTH), F32),
                   jax.ShapeDtypeStruct((M, POOL_WIDTH), BF16),
                   jax.ShapeDtypeStruct((POOL_PAST, M, POOL_WIDTH), F32)),
        compiler_params=_params(("arbitrary",)),
        name="in_proj_sample",
    )(x, g, w_in, st_t, pool_w, pool_scale)


def _paged_attn_kernel(pt_ref, lamp_ref, q_ref, kn_ref, vn_ref, bias_ref, *rest, n_pages, lam0):
    del pt_ref
    k_refs = rest[:n_pages]
    v_refs = rest[n_pages:2 * n_pages]
    o_ref = rest[2 * n_pages]
    s_ref = rest[2 * n_pages + 1]
    rows = PAGE * N_HEADS
    nt_dims = (((1,), (1,)), ((), ()))

    q4 = q_ref[...] * (DIFF_HD ** -0.5)
    lane = lax.broadcasted_iota(jnp.int32, (N_HEADS, HEAD_W), 1)
    q_all = jnp.concatenate([jnp.where(lane < DIFF_HD, q4, 0.0),
                             jnp.where(lane >= DIFF_HD, q4, 0.0)], axis=0)
    q_all_b = q_all.astype(BF16)
    for p in range(n_pages):
        kp = k_refs[p][...].astype(BF16)
        s_ref[:, p * rows:(p + 1) * rows] = lax.dot_general(q_all_b, kp, nt_dims, preferred_element_type=F32)

    s = s_ref[...] + bias_ref[...]
    kn8 = jnp.concatenate([kn_ref[...], kn_ref[...]], axis=0)
    s_new = jnp.sum(q_all * kn8, axis=1, keepdims=True)
    m = jnp.maximum(jnp.max(s, axis=1, keepdims=True), s_new)
    pr = jnp.exp(s - m)
    pr_new = jnp.exp(s_new - m)
    l = jnp.sum(pr, axis=1, keepdims=True) + pr_new
    lam = _lambda_value(lamp_ref, lam0)
    rowi = lax.broadcasted_iota(jnp.int32, (2 * N_HEADS, 1), 0)
    coef = jnp.where(rowi < N_HEADS, 1.0, -lam) / l
    w = (pr * coef).astype(BF16)

    vn8 = jnp.concatenate([vn_ref[...], vn_ref[...]], axis=0)
    acc = (pr_new * coef) * vn8
    for p in range(n_pages):
        vp = v_refs[p][...].astype(BF16)
        acc = acc + jnp.dot(w[:, p * rows:(p + 1) * rows], vp, preferred_element_type=F32)
    o_ref[...] = acc[:N_HEADS] + acc[N_HEADS:]


def _paged_attn(zs, page_table, cache_k, cache_v, layer, lamp, lam0):
    Bs = zs.shape[0]
    n_pages = page_table.shape[1]
    n_past = n_pages * PAGE
    depth, n_phys = cache_k.shape[0], cache_k.shape[1]
    rows = PAGE * N_HEADS
    ck = cache_k.reshape(depth * n_phys, rows, HEAD_W)
    cv = cache_v.reshape(depth * n_phys, rows, HEAD_W)
    heads3 = lambda lo: zs[:, lo:lo + QK_WIDTH].reshape(Bs, N_HEADS, HEAD_W)
    q3 = heads3(POOL_WIDTH)
    kn3 = heads3(POOL_WIDTH + QK_WIDTH)
    vn3 = heads3(POOL_WIDTH + 2 * QK_WIDTH)

    slopes = 2.0 ** (-8.0 * np.arange(1, N_HEADS + 1) / N_HEADS)
    key_row = np.arange(n_past * N_HEADS)
    key_pos, key_head = key_row // N_HEADS, key_row % N_HEADS
    row_head = np.arange(2 * N_HEADS) % N_HEADS
    alibi = -slopes[row_head][:, None] * (n_past - key_pos)[None, :].astype(np.float64)
    bias = jnp.asarray(np.where(row_head[:, None] == key_head[None, :], alibi, -np.inf), F32)

    def page_spec(p):
        return pl.BlockSpec((None, rows, HEAD_W), lambda b, pt: (layer * n_phys + pt[b, p], 0, 0))

    const2 = lambda b, pt: (0, 0)
    row3 = pl.BlockSpec((None, N_HEADS, HEAD_W), lambda b, pt: (b, 0, 0))
    grid_spec = pltpu.PrefetchScalarGridSpec(
        num_scalar_prefetch=1,
        grid=(Bs,),
        in_specs=[pl.BlockSpec((4, DIFF_HD), const2), row3, row3, row3,
                  pl.BlockSpec((2 * N_HEADS, n_past * N_HEADS), const2)]
                 + [page_spec(p) for p in range(n_pages)]
                 + [page_spec(p) for p in range(n_pages)],
        out_specs=row3,
        scratch_shapes=[pltpu.VMEM((2 * N_HEADS, n_past * N_HEADS), F32)],
    )
    return pl.pallas_call(
        functools.partial(_paged_attn_kernel, n_pages=n_pages, lam0=lam0),
        grid_spec=grid_spec,
        out_shape=jax.ShapeDtypeStruct((Bs, N_HEADS, HEAD_W), F32),
        compiler_params=_params(("arbitrary",)),
        name="paged_diff_attn",
    )(page_table, lamp, q3, kn3, vn3, bias, *([ck] * n_pages), *([cv] * n_pages))


def _cross_sample_kernel(q_ref, k_ref, v_ref, o_ref):
    n_rows = N_MEM * N_CROSS_HEADS * 2
    groups = 2 * N_CROSS_HEADS
    nt_dims = (((1,), (1,)), ((), ()))
    row = lax.broadcasted_iota(jnp.int32, (groups, n_rows), 0)
    lane = lax.broadcasted_iota(jnp.int32, (groups, n_rows), 1)
    own_lane = (lane & (groups - 1)) == row
    head_lane = (lane & (groups - 1)) == (row & (N_CROSS_HEADS - 1))
    for i in range(q_ref.shape[0]):
        s8 = lax.dot_general(q_ref[i], k_ref[i].astype(BF16), nt_dims, preferred_element_type=F32)
        part = jnp.sum(jnp.where(own_lane, s8, 0.0), axis=0, keepdims=True)
        full = part + pltpu.roll(part, n_rows - N_CROSS_HEADS, axis=1)
        s = jnp.where(head_lane, jnp.broadcast_to(full, (groups, n_rows)), NEG_INF)
        m = jnp.max(s, axis=1, keepdims=True)
        p = jnp.exp(s - m)
        a = p / jnp.sum(p, axis=1, keepdims=True)
        w = jnp.where(row < N_CROSS_HEADS, a, pltpu.roll(a, N_CROSS_HEADS, axis=1))
        o_ref[i] = jnp.dot(w.astype(BF16), v_ref[i].astype(BF16),
                           preferred_element_type=F32).astype(o_ref.dtype)


def _cross_sample(qc, mem_k, mem_v, layer):
    Bs = qc.shape[0]
    depth = mem_k.shape[0]
    half = CROSS_HD // 2
    n_rows = N_MEM * N_CROSS_HEADS * 2

    def stored_rows(mem):
        m6 = mem.reshape(depth, Bs, N_MEM, N_CROSS_HEADS, 2, half)
        return jnp.transpose(m6, (0, 1, 2, 4, 3, 5)).reshape(depth * Bs, n_rows, half)

    q8 = jnp.transpose(qc.reshape(Bs, N_CROSS_HEADS, 2, half), (0, 2, 1, 3)).reshape(Bs, 2 * N_CROSS_HEADS, half)
    nb = CROSS_ROWS_PER_STEP
    mem_blk = pl.BlockSpec((nb, n_rows, half), lambda b: (layer * (Bs // nb) + b, 0, 0))
    blk = lambda b: (b, 0, 0)
    o8 = pl.pallas_call(
        _cross_sample_kernel,
        grid=(Bs // nb,),
        in_specs=[pl.BlockSpec((nb, 2 * N_CROSS_HEADS, half), blk), mem_blk, mem_blk],
        out_specs=pl.BlockSpec((nb, 2 * N_CROSS_HEADS, half), blk),
        out_shape=jax.ShapeDtypeStruct((Bs, 2 * N_CROSS_HEADS, half), BF16),
        compiler_params=_params(("arbitrary",)),
        name="cross_sample",
    )(q8, stored_rows(mem_k), stored_rows(mem_v))
    return jnp.transpose(o8.reshape(Bs, 2, N_CROSS_HEADS, half), (0, 2, 1, 3)).reshape(Bs, D_MODEL)


def kernel(x_prompt, x_sample, mem_prompt, cache_k, cache_v, cache_mem_k, cache_mem_v, state_pool, state_conv, page_table, g_mix, w_in, pool_w, pool_scale, lam_q1, lam_k1, lam_q2, lam_k2, subln_g, w_out, g_cross, wq_c, wk_c, wv_c, wo_c, g_ffn, w_up, conv_w, conv_b, w_down, g_final):
    depth = w_in.shape[0]
    Bp, T, D = x_prompt.shape
    Bs = x_sample.shape[0]
    W2 = 2 * D_FF
    bf = lambda a: a.astype(BF16)
    w_in_b, pool_w_b, w_out_b = bf(w_in), bf(pool_w), bf(w_out)
    wq_b, wk_b, wv_b, wo_b, w_up_b, w_down_b = bf(wq_c), bf(wk_c), bf(wv_c), bf(wo_c), bf(w_up), bf(w_down)
    gfin = g_final.reshape(1, D)

    xp = x_prompt
    xs = x_sample.reshape(Bs, D)
    outs = {n: [] for n in ("kp", "vp", "mkp", "mvp", "pp", "cp", "ks", "vs", "ps", "cs")}
    for l in range(depth):
        lam0 = _lambda_init(l)
        last = l == depth - 1
        g_mix_l = g_mix[l].reshape(1, D)
        g_cross_l = g_cross[l].reshape(1, D)
        g_ffn_l = g_ffn[l].reshape(1, D)
        ps_l = pool_scale[l].reshape(1, POOL_WIDTH)
        cb_l = conv_b[l].reshape(1, W2)
        lamp = jnp.stack([lam_q1[l], lam_k1[l], lam_q2[l], lam_k2[l]])

        mk, mv, mkt, mvb = _mem_kv(mem_prompt, wk_b[l], wv_b[l])
        kp, vp, qt, kb, vt, po, tail = _in_proj_prompt(xp, g_mix_l, w_in_b[l], pool_w_b[l], ps_l)
        o = _flash_diff_attn(qt, kb, vt, lamp, subln_g[l].reshape(HEAD_W, 1), lam0)
        x1, qc = _out_q(xp.reshape(Bp * T, D), po.reshape(Bp * T, POOL_WIDTH), o.reshape(Bp * T, -1),
                        subln_g[l].reshape(1, HEAD_W), w_out_b[l], g_cross_l, wq_b[l], ROW_TILE)
        oc = _cross_prompt(qc.reshape(Bp, T, D), mkt, mvb, ROW_TILE)
        xp, up_tail = _ffn(x1.reshape(Bp, T, D), oc, None, wo_b[l], g_ffn_l, w_up_b[l], conv_w[l], cb_l,
                           w_down_b[l], gfin, ROW_TILE, last)
        outs["kp"].append(kp.reshape(Bp, T, N_HEADS, HEAD_W))
        outs["vp"].append(vp.reshape(Bp, T, N_HEADS, HEAD_W))
        outs["mkp"].append(mk.reshape(Bp, N_MEM, N_CROSS_HEADS, CROSS_HD))
        outs["mvp"].append(mv.reshape(Bp, N_MEM, N_CROSS_HEADS, CROSS_HD))
        outs["pp"].append(tail[:, 2 * SUBLANES - POOL_PAST:])
        outs["cp"].append(up_tail[:, SUBLANES - (CONV_W - 1):])

        st_t = jnp.swapaxes(state_pool[l], 0, 1)
        zs, pos, nst = _in_proj_sample(xs, g_mix_l, w_in_b[l], st_t, pool_w_b[l], ps_l)
        o_raw = _paged_attn(zs, page_table, cache_k, cache_v, l, lamp, lam0)
        x1s, qcs = _out_q(xs, pos, o_raw.reshape(Bs, -1), subln_g[l].reshape(1, HEAD_W), w_out_b[l],
                          g_cross_l, wq_b[l], Bs, subln_scale=1.0 - lam0)
        ocs = _cross_sample(qcs, cache_mem_k, cache_mem_v, l)
        prev = (state_conv[l][:, 0].reshape(1, Bs, W2), state_conv[l][:, 1].reshape(1, Bs, W2))
        xs3, up_s = _ffn(x1s.reshape(1, Bs, D), ocs.reshape(1, Bs, D), prev, wo_b[l], g_ffn_l, w_up_b[l],
                         conv_w[l], cb_l, w_down_b[l], gfin, Bs, last)
        xs = xs3.reshape(Bs, D)
        outs["ks"].append(zs[:, POOL_WIDTH + QK_WIDTH:POOL_WIDTH + 2 * QK_WIDTH].reshape(Bs, 1, N_HEADS, HEAD_W))
        outs["vs"].append(zs[:, POOL_WIDTH + 2 * QK_WIDTH:].reshape(Bs, 1, N_HEADS, HEAD_W))
        outs["ps"].append(jnp.swapaxes(nst, 0, 1))
        outs["cs"].append(jnp.stack([state_conv[l][:, 1], up_s.reshape(Bs, W2)], axis=1))

    st = lambda n: jnp.stack(outs[n])
    return (xp, xs.reshape(Bs, 1, D), st("kp"), st("vp"), st("mkp"), st("mvp"), st("pp"), st("cp"),
            st("ks"), st("vs"), st("ps"), st("cs"))
```

```python
import functools
import math

import jax
import jax.numpy as jnp
import numpy as np
from jax import lax
from jax.experimental import pallas as pl
from jax.experimental.pallas import tpu as pltpu

F32 = jnp.float32
BF16 = jnp.bfloat16

D_MODEL = 1024
POOL_WIDTH = 512
POOL_WINDOWS = (2, 4, 8, 16)
POOL_GROUP = 128
POOL_PAST = 15
N_HEADS = 4
HEAD_W = 128
DIFF_HD = 64
QK_WIDTH = 512
IN_WIDTH = 2048
N_MEM = 256
N_CROSS_HEADS = 4
CROSS_HD = 256
D_FF = 2816
CONV_W = 3
EPS = 1e-6
SUBLN_EPS = 1e-5
PAGE = 128

LANES = 128
SUBLANES = 8
VMEM_LIMIT = 56 * 1024 * 1024

ROW_TILE = 512
TQ = 256
TK = 256
FF_CHUNK = 256
CROSS_ROWS_PER_STEP = 2
NEG_INF = float("-inf")


def _lambda_init(layer_idx):
    return 0.8 - 0.6 * math.exp(-0.3 * layer_idx)


def _rms(x, g, eps):
    ms = jnp.mean(x * x, axis=-1, keepdims=True)
    return x * lax.rsqrt(ms + eps) * g


def _params(sem, vmem=VMEM_LIMIT):
    return pltpu.CompilerParams(dimension_semantics=sem, vmem_limit_bytes=vmem)


def _const_spec(shape):
    nd = len(shape)
    return pl.BlockSpec(shape, lambda *_: (0,) * nd)


def _pool_groups(u, shifted, cnt_fn, pw_ref, ps_ref, po_ref):
    for g, w in enumerate(POOL_WINDOWS):
        cols = slice(g * POOL_GROUP, (g + 1) * POOL_GROUP)
        ug = u[:, cols]
        ws = ug
        for j in range(1, w):
            ws = ws + shifted(j, cols)
        p = ws / cnt_fn(w) - ug
        y = jnp.dot(p.astype(BF16), pw_ref[g], preferred_element_type=F32)
        po_ref[:, cols] = (y * ps_ref[:, cols]).astype(po_ref.dtype)


def _in_proj_prompt_kernel(x_ref, g_ref, w_ref, pw_ref, ps_ref,
                           k_ref, v_ref, qt_ref, kb_ref, vt_ref, po_ref, tail_ref, ext_ref, *, tm):
    t = pl.program_id(1)
    halo = 2 * SUBLANES

    @pl.when(t == 0)
    def _():
        ext_ref[0:halo, :] = jnp.zeros((halo, POOL_WIDTH), F32)

    h = _rms(x_ref[...], g_ref[...], EPS).astype(BF16)
    z = jnp.dot(h, w_ref[...], preferred_element_type=F32)
    u = z[:, :POOL_WIDTH]
    scale = DIFF_HD ** -0.5 * LOG2E
    for hh in range(N_HEADS):
        q0 = POOL_WIDTH + hh * HEAD_W
        k0 = POOL_WIDTH + QK_WIDTH + hh * HEAD_W
        v0 = POOL_WIDTH + 2 * QK_WIDTH + hh * HEAD_W
        k_ref[pl.ds(hh, tm, stride=N_HEADS), :] = z[:, k0:k0 + HEAD_W]
        v_ref[pl.ds(hh, tm, stride=N_HEADS), :] = z[:, v0:v0 + HEAD_W]
        for s in range(tm // TQ):
            rows = slice(s * TQ, (s + 1) * TQ)
            qt_ref[hh, s] = (z[rows, q0:q0 + HEAD_W] * scale).T.astype(BF16)
            vt_ref[hh, s] = z[rows, v0:v0 + HEAD_W].T.astype(BF16)
            kb_ref[hh, s] = z[rows, k0:k0 + HEAD_W].astype(BF16)

    ext_ref[halo:halo + tm, :] = u
    pos = t * tm + lax.broadcasted_iota(jnp.int32, (tm, POOL_GROUP), 0)

    def shifted(j, cols):
        return ext_ref[halo - j:halo - j + tm, cols]

    def cnt(w):
        return jnp.minimum(pos + 1, w).astype(F32)

    _pool_groups(u, shifted, cnt, pw_ref, ps_ref, po_ref)
    tail_ref[...] = u[tm - halo:, :]
    ext_ref[0:halo, :] = u[tm - halo:, :]


def _in_proj_prompt(x, g, w_in, pool_w, pool_scale):
    B, T, D = x.shape
    tm = ROW_TILE
    nt = T // tm
    spt = tm // TQ
    row = lambda b, t: (b, t, 0)
    out_shape = (
        jax.ShapeDtypeStruct((B, T * N_HEADS, HEAD_W), F32),
        jax.ShapeDtypeStruct((B, T * N_HEADS, HEAD_W), F32),
        jax.ShapeDtypeStruct((B, N_HEADS, T // TQ, HEAD_W, TQ), BF16),
        jax.ShapeDtypeStruct((B, N_HEADS, T // TK, TK, HEAD_W), BF16),
        jax.ShapeDtypeStruct((B, N_HEADS, T // TK, HEAD_W, TK), BF16),
        jax.ShapeDtypeStruct((B, T, POOL_WIDTH), BF16),
        jax.ShapeDtypeStruct((B, 2 * SUBLANES, POOL_WIDTH), F32),
    )
    tile5 = lambda b, t: (b, 0, t, 0, 0)
    return pl.pallas_call(
        functools.partial(_in_proj_prompt_kernel, tm=tm),
        grid=(B, nt),
        in_specs=[
            pl.BlockSpec((None, tm, D), row),
            _const_spec((1, D)),
            _const_spec((D, IN_WIDTH)),
            _const_spec((len(POOL_WINDOWS), POOL_GROUP, POOL_GROUP)),
            _const_spec((1, POOL_WIDTH)),
        ],
        out_specs=(
            pl.BlockSpec((None, tm * N_HEADS, HEAD_W), row),
            pl.BlockSpec((None, tm * N_HEADS, HEAD_W), row),
            pl.BlockSpec((None, N_HEADS, spt, HEAD_W, TQ), tile5),
            pl.BlockSpec((None, N_HEADS, spt, TK, HEAD_W), tile5),
            pl.BlockSpec((None, N_HEADS, spt, HEAD_W, TK), tile5),
            pl.BlockSpec((None, tm, POOL_WIDTH), row),
            pl.BlockSpec((None, 2 * SUBLANES, POOL_WIDTH), lambda b, t: (b, 0, 0)),
        ),
        out_shape=out_shape,
        scratch_shapes=[pltpu.VMEM((tm + 2 * SUBLANES, POOL_WIDTH), F32)],
        compiler_params=_params(("arbitrary", "arbitrary")),
        name="in_proj_prompt",
    )(x, g, w_in, pool_w, pool_scale)


def _lambda_value(lamp_ref, lam0):
    lp = lamp_ref[...]
    s1 = jnp.sum(lp[0:1] * lp[1:2], axis=1, keepdims=True)
    s2 = jnp.sum(lp[2:3] * lp[3:4], axis=1, keepdims=True)
    return jnp.exp(s1) - jnp.exp(s2) + lam0


def _bf16_terms(x, n):
    terms = []
    for _ in range(n):
        t = float(np.asarray(x, np.float32).astype(jnp.bfloat16).astype(np.float32))
        terms.append(t)
        x -= t
    return terms


LOG2E = math.log2(math.e)
LOG2E_TERMS = _bf16_terms(LOG2E, 3)


def _flash_kernel(lamp_ref, g_ref, qt_ref, k_ref, vt_ref, o_ref, rhs_ref, m_ref, acc_ref, p_ref, alpha_ref,
                  *, lam0):
    i = pl.program_id(1)
    two_tq = 2 * TQ
    n_terms = len(LOG2E_TERMS)
    slopes = [2.0 ** (-8.0 * (hh + 1) / N_HEADS) for hh in range(N_HEADS)]
    log2e_hi = float(np.float32(LOG2E))
    log2e_lo = LOG2E - log2e_hi

    z64 = jnp.zeros((DIFF_HD, TQ), BF16)
    row = lax.broadcasted_iota(jnp.int32, (HEAD_W, two_tq), 0)
    col = lax.broadcasted_iota(jnp.int32, (HEAD_W, two_tq), 1)
    r_f = (col & (TQ - 1)).astype(F32)
    for hh in range(N_HEADS):
        qt = qt_ref[hh]
        top = jnp.concatenate([qt[:DIFF_HD], z64], axis=1)
        mid = jnp.concatenate([z64, qt[DIFF_HD:]], axis=1)
        bot = jnp.where((row >= n_terms) & (row < 2 * n_terms), slopes[hh] * r_f, 0.0)
        for t, e_t in enumerate(LOG2E_TERMS):
            bot = jnp.where(row == t, slopes[hh] * e_t, bot)
        rhs_ref[hh] = jnp.concatenate([top, mid, bot.astype(BF16)], axis=0)
    m_ref[...] = jnp.full(m_ref.shape, NEG_INF, F32)
    acc_ref[...] = jnp.zeros(acc_ref.shape, F32)
    p_ref[...] = jnp.zeros(p_ref.shape, BF16)
    alpha_ref[...] = jnp.ones(alpha_ref.shape, F32)

    krow = lax.broadcasted_iota(jnp.int32, (TK, HEAD_W), 0)
    klane = lax.broadcasted_iota(jnp.int32, (TK, HEAD_W), 1)
    aug_k = jnp.where(klane < n_terms, krow.astype(F32), 0.0)
    for t, e_t in enumerate(LOG2E_TERMS):
        aug_k = jnp.where(klane == n_terms + t, -e_t, aug_k)
    aug_k = aug_k.astype(BF16)
    def score_tiles(tiles):
        scores = []
        for hh in range(N_HEADS):
            per_tile = []
            for j, masked in tiles:
                kk = jnp.concatenate([k_ref[hh, j], aug_k], axis=1)
                s = jnp.dot(kk, rhs_ref[hh], preferred_element_type=F32)
                if masked:
                    c_idx = lax.broadcasted_iota(jnp.int32, (TK, two_tq), 0)
                    r_idx = lax.broadcasted_iota(jnp.int32, (TK, two_tq), 1) & (TQ - 1)
                    s = jnp.where(c_idx <= r_idx, s, NEG_INF)
                per_tile.append(s)
            scores.append(per_tile)
        return scores

    def softmax_tiles(scores, tiles):
        probs = []
        for hh in range(N_HEADS):
            dlts = []
            for j, _ in tiles:
                sx = slopes[hh] * ((i - j) * TQ).astype(F32)
                dlts.append(sx * log2e_hi + sx * log2e_lo)
            m_old = m_ref[hh]
            m_new = m_old
            for s, dlt in zip(scores[hh], dlts):
                m_new = jnp.maximum(m_new, jnp.max(s, axis=0, keepdims=True) - dlt)
            parts = [jnp.exp2(s - (m_new + dlt)).astype(BF16) for s, dlt in zip(scores[hh], dlts)]
            p = parts[0] if len(parts) == 1 else jnp.concatenate(parts, axis=0)
            probs.append((p, jnp.exp2(m_old - m_new)))
            m_ref[hh] = m_new
        return probs

    def accumulate(v_tiles, probs):
        ones_rows = jnp.ones((2 * SUBLANES, TK * len(v_tiles)), BF16)
        for hh in range(N_HEADS):
            p, alpha = probs[hh]
            vts = [vt_ref[hh, j] for j in v_tiles]
            vt = vts[0] if len(vts) == 1 else jnp.concatenate(vts, axis=1)
            v1 = jnp.concatenate([vt, ones_rows], axis=0)
            pv = jnp.dot(v1, p, preferred_element_type=F32)
            acc_ref[hh] = alpha * acc_ref[hh] + pv

    def previous_probs():
        return [(p_ref[hh], alpha_ref[hh]) for hh in range(N_HEADS)]

    def body(jj, carry):
        ja = 2 * jj
        tiles = [(ja, False), (ja + 1, False)]
        scores = score_tiles(tiles)
        pa = jnp.maximum(ja - 2, 0)
        accumulate([pa, pa + 1], previous_probs())
        for hh, (p, alpha) in enumerate(softmax_tiles(scores, tiles)):
            p_ref[hh] = p
            alpha_ref[hh] = alpha
        return carry

    n_pairs = lax.shift_right_logical(i, 1)
    lax.fori_loop(0, n_pairs, body, 0)
    last_pair = jnp.maximum(2 * n_pairs - 2, 0)

    def tail(tiles):
        scores = score_tiles(tiles)
        accumulate([last_pair, last_pair + 1], previous_probs())
        accumulate([j for j, _ in tiles], softmax_tiles(scores, tiles))

    odd = (i & 1) == 1

    @pl.when(odd)
    def _():
        tail([(i - 1, False), (i, True)])

    @pl.when(jnp.logical_not(odd))
    def _():
        tail([(i, True)])


    lam = _lambda_value(lamp_ref, lam0)
    for hh in range(N_HEADS):
        acc = acc_ref[hh]
        o1 = acc[:HEAD_W, :TQ] / acc[HEAD_W:HEAD_W + 1, :TQ]
        o2 = acc[:HEAD_W, TQ:] / acc[HEAD_W:HEAD_W + 1, TQ:]
        ot = o1 - lam * o2
        ms = jnp.mean(ot * ot, axis=0, keepdims=True)
        y = ot * lax.rsqrt(ms + SUBLN_EPS) * g_ref[...] * (1.0 - lam0)
        o_ref[:, hh * HEAD_W:(hh + 1) * HEAD_W] = y.T.astype(o_ref.dtype)


def _flash_diff_attn(qt, kb, vt, lamp, subln_g, lam0):
    B, H, nq = qt.shape[0], qt.shape[1], qt.shape[2]
    nk = kb.shape[2]
    T = nq * TQ
    whole = lambda b, i: (b, 0, 0, 0, 0)
    return pl.pallas_call(
        functools.partial(_flash_kernel, lam0=lam0),
        grid=(B, nq),
        in_specs=[
            _const_spec((4, DIFF_HD)),
            _const_spec((HEAD_W, 1)),
            pl.BlockSpec((None, H, None, HEAD_W, TQ), lambda b, i: (b, 0, i, 0, 0)),
            pl.BlockSpec((None, H, nk, TK, HEAD_W), whole, pipeline_mode=pl.Buffered(1)),
            pl.BlockSpec((None, H, nk, HEAD_W, TK), whole, pipeline_mode=pl.Buffered(1)),
        ],
        out_specs=pl.BlockSpec((None, TQ, H * HEAD_W), lambda b, i: (b, i, 0)),
        out_shape=jax.ShapeDtypeStruct((B, T, H * HEAD_W), BF16),
        scratch_shapes=[
            pltpu.VMEM((H, 2 * HEAD_W, 2 * TQ), BF16),
            pltpu.VMEM((H, 1, 2 * TQ), F32),
            pltpu.VMEM((H, HEAD_W + 2 * SUBLANES, 2 * TQ), F32),
            pltpu.VMEM((H, 2 * TK, 2 * TQ), BF16),
            pltpu.VMEM((H, 1, 2 * TQ), F32),
        ],
        compiler_params=_params(("arbitrary", "arbitrary")),
        name="flash_diff_attn",
    )(lamp, subln_g, qt, kb, vt)


def _out_q_kernel(x_ref, po_ref, o_ref, sg_ref, wout_ref, gc_ref, wq_ref, x1_ref, qc_ref, *, subln_scale):
    o = o_ref[...]
    if subln_scale is not None:
        parts = []
        for hh in range(N_HEADS):
            oh = o[:, hh * HEAD_W:(hh + 1) * HEAD_W]
            parts.append(_rms(oh, sg_ref[...], SUBLN_EPS) * subln_scale)
        o = jnp.concatenate(parts, axis=1)
    mix = jnp.dot(po_ref[...], wout_ref[0:POOL_WIDTH, :], preferred_element_type=F32)
    mix = mix + jnp.dot(o.astype(BF16), wout_ref[POOL_WIDTH:, :], preferred_element_type=F32)
    x1 = x_ref[...] + mix
    x1_ref[...] = x1
    hc = _rms(x1, gc_ref[...], EPS).astype(BF16)
    qc = jnp.dot(hc, wq_ref[...], preferred_element_type=F32) * (CROSS_HD ** -0.5)
    qc_ref[...] = qc.astype(BF16)


def _out_q(x, po, o, subln_g_row, w_out, g_cross, wq_c, tm, subln_scale=None):
    M, D = x.shape
    row = lambda t: (t, 0)
    return pl.pallas_call(
        functools.partial(_out_q_kernel, subln_scale=subln_scale),
        grid=(M // tm,),
        in_specs=[
            pl.BlockSpec((tm, D), row),
            pl.BlockSpec((tm, POOL_WIDTH), row),
            pl.BlockSpec((tm, N_HEADS * HEAD_W), row),
            _const_spec((1, HEAD_W)),
            _const_spec((D, D)),
            _const_spec((1, D)),
            _const_spec((D, D)),
        ],
        out_specs=(pl.BlockSpec((tm, D), row), pl.BlockSpec((tm, D), row)),
        out_shape=(jax.ShapeDtypeStruct((M, D), F32), jax.ShapeDtypeStruct((M, D), BF16)),
        compiler_params=_params(("arbitrary",)),
        name="out_q",
    )(x, po, o, subln_g_row, w_out, g_cross, wq_c)


def _mem_kv_kernel(mem_ref, wk_ref, wv_ref, mk_ref, mv_ref, mkt_ref, mvb_ref):
    m = mem_ref[...].astype(BF16)
    k = jnp.dot(m, wk_ref[...], preferred_element_type=F32)
    v = jnp.dot(m, wv_ref[...], preferred_element_type=F32)
    mk_ref[...] = k
    mv_ref[...] = v
    mkt_ref[...] = k.T.astype(BF16)
    mvb_ref[...] = v.astype(BF16)


def _mem_kv(mem, wk, wv):
    B, N, D = mem.shape
    blk = lambda b: (b, 0, 0)
    return pl.pallas_call(
        _mem_kv_kernel,
        grid=(B,),
        in_specs=[pl.BlockSpec((None, N, D), blk), _const_spec((D, D)), _const_spec((D, D))],
        out_specs=(pl.BlockSpec((None, N, D), blk), pl.BlockSpec((None, N, D), blk),
                   pl.BlockSpec((None, D, N), blk), pl.BlockSpec((None, N, D), blk)),
        out_shape=(jax.ShapeDtypeStruct((B, N, D), F32), jax.ShapeDtypeStruct((B, N, D), F32),
                   jax.ShapeDtypeStruct((B, D, N), BF16), jax.ShapeDtypeStruct((B, N, D), BF16)),
        compiler_params=_params(("arbitrary",)),
        name="mem_kv",
    )(mem, wk, wv)


def _cross_prompt_kernel(q_ref, kt_ref, v_ref, o_ref):
    for hh in range(N_CROSS_HEADS):
        cols = slice(hh * CROSS_HD, (hh + 1) * CROSS_HD)
        s = jnp.dot(q_ref[:, cols], kt_ref[cols, :], preferred_element_type=F32)
        m = jnp.max(s, axis=1, keepdims=True)
        p = jnp.exp(s - m)
        l = jnp.sum(p, axis=1, keepdims=True)
        o = jnp.dot(p.astype(BF16), v_ref[:, cols], preferred_element_type=F32) / l
        o_ref[:, cols] = o.astype(o_ref.dtype)


def _cross_prompt(qc, mkt, mvb, tm):
    B, T, D = qc.shape
    row = lambda b, t: (b, t, 0)
    blk = lambda b, t: (b, 0, 0)
    return pl.pallas_call(
        _cross_prompt_kernel,
        grid=(B, T // tm),
        in_specs=[pl.BlockSpec((None, tm, D), row),
                  pl.BlockSpec((None, D, N_MEM), blk),
                  pl.BlockSpec((None, N_MEM, D), blk)],
        out_specs=pl.BlockSpec((None, tm, D), row),
        out_shape=jax.ShapeDtypeStruct((B, T, D), BF16),
        compiler_params=_params(("arbitrary", "arbitrary")),
        name="cross_prompt",
    )(qc, mkt, mvb)


def _ffn_kernel(*refs, seq_mode, final_norm, tm):
    if seq_mode:
        (x1_ref, oc_ref, wo_ref, g_ref, wup_ref, cw_ref, cb_ref, wdn_ref, gfin_ref,
         out_ref, up_out_ref, x2_ref, h_ref, act_ref, halo_ref) = refs
    else:
        (x1_ref, oc_ref, p2_ref, p1_ref, wo_ref, g_ref, wup_ref, cw_ref, cb_ref, wdn_ref, gfin_ref,
         out_ref, up_out_ref, x2_ref, h_ref, act_ref) = refs
    t = pl.program_id(1)

    if seq_mode:
        @pl.when(t == 0)
        def _():
            halo_ref[...] = jnp.zeros(halo_ref.shape, F32)

    x2 = x1_ref[...] + jnp.dot(oc_ref[...], wo_ref[...], preferred_element_type=F32)
    x2_ref[...] = x2
    h_ref[...] = _rms(x2, g_ref[...], EPS).astype(BF16)

    def up_proj(c0):
        cols_a = slice(c0, c0 + FF_CHUNK)
        cols_g = slice(D_FF + c0, D_FF + c0 + FF_CHUNK)
        return (jnp.dot(h_ref[...], wup_ref[:, cols_a], preferred_element_type=F32),
                jnp.dot(h_ref[...], wup_ref[:, cols_g], preferred_element_type=F32))

    def conv_part(up, col0):
        cols = slice(col0, col0 + FF_CHUNK)
        if seq_mode:
            ext = jnp.concatenate([halo_ref[:, cols], up], axis=0)
            um1 = ext[SUBLANES - 1:SUBLANES - 1 + tm]
            um2 = ext[SUBLANES - 2:SUBLANES - 2 + tm]
            halo_ref[:, cols] = up[tm - SUBLANES:, :]
            up_out_ref[:, cols] = up[tm - SUBLANES:, :]
        else:
            um1 = p1_ref[:, cols]
            um2 = p2_ref[:, cols]
            up_out_ref[:, cols] = up
        c = cb_ref[:, cols] + um2 * cw_ref[0:1, cols]
        c = c + um1 * cw_ref[1:2, cols]
        return c + up * cw_ref[2:3, cols]

    starts = list(range(0, D_FF, FF_CHUNK))
    nxt = up_proj(starts[0])
    for n, c0 in enumerate(starts):
        up_a, up_g = nxt
        if n + 1 < len(starts):
            nxt = up_proj(starts[n + 1])
        a = conv_part(up_a, c0)
        g = conv_part(up_g, D_FF + c0)
        act_ref[:, c0:c0 + FF_CHUNK] = (g * (1.0 / (1.0 + jnp.exp(-g))) * a).astype(BF16)

    x3 = x2_ref[...] + jnp.dot(act_ref[...], wdn_ref[...], preferred_element_type=F32)
    if final_norm:
        x3 = _rms(x3, gfin_ref[...], EPS)
    out_ref[...] = x3


def _ffn(x1, oc, prev, wo, g_ffn, w_up, conv_w, conv_b, w_down, g_final, tm, final_norm):
    B, T, D = x1.shape
    seq_mode = prev is None
    W2 = 2 * D_FF
    row = lambda b, t: (b, t, 0)
    in_specs = [pl.BlockSpec((None, tm, D), row), pl.BlockSpec((None, tm, D), row)]
    args = [x1, oc]
    if not seq_mode:
        in_specs += [pl.BlockSpec((None, tm, W2), row), pl.BlockSpec((None, tm, W2), row)]
        args += list(prev)
    in_specs += [_const_spec((D, D)), _const_spec((1, D)), _const_spec((D, W2)),
                 _const_spec((CONV_W, W2)), _const_spec((1, W2)), _const_spec((D_FF, D)),
                 _const_spec((1, D))]
    args += [wo, g_ffn, w_up, conv_w, conv_b, w_down, g_final]
    scratch = [pltpu.VMEM((tm, D), F32), pltpu.VMEM((tm, D), BF16), pltpu.VMEM((tm, D_FF), BF16)]
    if seq_mode:
        up_rows = SUBLANES
        up_spec = pl.BlockSpec((None, SUBLANES, W2), lambda b, t: (b, 0, 0))
        scratch += [pltpu.VMEM((SUBLANES, W2), F32)]
    else:
        up_rows = T
        up_spec = pl.BlockSpec((None, tm, W2), row)
    return pl.pallas_call(
        functools.partial(_ffn_kernel, seq_mode=seq_mode, final_norm=final_norm, tm=tm),
        grid=(B, T // tm),
        in_specs=in_specs,
        out_specs=(pl.BlockSpec((None, tm, D), row), up_spec),
        out_shape=(jax.ShapeDtypeStruct((B, T, D), F32), jax.ShapeDtypeStruct((B, up_rows, W2), F32)),
        scratch_shapes=scratch,
        compiler_params=_params(("arbitrary", "arbitrary")),
        name="wo_ffn_seq" if seq_mode else "wo_ffn_step",
    )(*args)


def _in_proj_sample_kernel(x_ref, g_ref, w_ref, st_ref, pw_ref, ps_ref, z_ref, po_ref, nst_ref):
    h = _rms(x_ref[...], g_ref[...], EPS).astype(BF16)
    z = jnp.dot(h, w_ref[...], preferred_element_type=F32)
    z_ref[...] = z
    u = z[:, :POOL_WIDTH]

    def shifted(j, cols):
        return st_ref[POOL_PAST - j, :, cols]

    _pool_groups(u, shifted, lambda w: float(w), pw_ref, ps_ref, po_ref)
    for r in range(POOL_PAST - 1):
        nst_ref[r] = st_ref[r + 1]
    nst_ref[POOL_PAST - 1] = u


def _in_proj_sample(x, g, w_in, st_t, pool_w, pool_scale):
    M, D = x.shape
    return pl.pallas_call(
        _in_proj_sample_kernel,
        grid=(1,),
        in_specs=[_const_spec((M, D)), _const_spec((1, D)), _const_spec((D, IN_WIDTH)),
                  _const_spec((POOL_PAST, M, POOL_WIDTH)),
                  _const_spec((len(POOL_WINDOWS), POOL_GROUP, POOL_GROUP)),
                  _const_spec((1, POOL_WIDTH))],
        out_specs=(_const_spec((M, IN_WIDTH)), _const_spec((M, POOL_WIDTH)),
                   _const_spec((POOL_PAST, M, POOL_WIDTH))),
        out_shape=(jax.ShapeDtypeStruct((M, IN_WIDTH), F32),
                   jax.ShapeDtypeStruct((M, POOL_WIDTH), BF16),
                   jax.ShapeDtypeStruct((POOL_PAST, M, POOL_WIDTH), F32)),
        compiler_params=_params(("arbitrary",)),
        name="in_proj_sample",
    )(x, g, w_in, st_t, pool_w, pool_scale)


def _paged_attn_kernel(pt_ref, lamp_ref, q_ref, kn_ref, vn_ref, bias_ref, *rest, n_pages, lam0):
    del pt_ref
    k_refs = rest[:n_pages]
    v_refs = rest[n_pages:2 * n_pages]
    o_ref = rest[2 * n_pages]
    s_ref = rest[2 * n_pages + 1]
    rows = PAGE * N_HEADS
    nt_dims = (((1,), (1,)), ((), ()))

    q4 = q_ref[...] * (DIFF_HD ** -0.5)
    lane = lax.broadcasted_iota(jnp.int32, (N_HEADS, HEAD_W), 1)
    q_all = jnp.concatenate([jnp.where(lane < DIFF_HD, q4, 0.0),
                             jnp.where(lane >= DIFF_HD, q4, 0.0)], axis=0)
    q_all_b = q_all.astype(BF16)
    for p in range(n_pages):
        kp = k_refs[p][...].astype(BF16)
        s_ref[:, p * rows:(p + 1) * rows] = lax.dot_general(q_all_b, kp, nt_dims, preferred_element_type=F32)

    s = s_ref[...] + bias_ref[...]
    kn8 = jnp.concatenate([kn_ref[...], kn_ref[...]], axis=0)
    s_new = jnp.sum(q_all * kn8, axis=1, keepdims=True)
    m = jnp.maximum(jnp.max(s, axis=1, keepdims=True), s_new)
    pr = jnp.exp(s - m)
    pr_new = jnp.exp(s_new - m)
    l = jnp.sum(pr, axis=1, keepdims=True) + pr_new
    lam = _lambda_value(lamp_ref, lam0)
    rowi = lax.broadcasted_iota(jnp.int32, (2 * N_HEADS, 1), 0)
    coef = jnp.where(rowi < N_HEADS, 1.0, -lam) / l
    w = (pr * coef).astype(BF16)

    vn8 = jnp.concatenate([vn_ref[...], vn_ref[...]], axis=0)
    acc = (pr_new * coef) * vn8
    for p in range(n_pages):
        vp = v_refs[p][...].astype(BF16)
        acc = acc + jnp.dot(w[:, p * rows:(p + 1) * rows], vp, preferred_element_type=F32)
    o_ref[...] = acc[:N_HEADS] + acc[N_HEADS:]


def _paged_attn(zs, page_table, cache_k, cache_v, layer, lamp, lam0):
    Bs = zs.shape[0]
    n_pages = page_table.shape[1]
    n_past = n_pages * PAGE
    depth, n_phys = cache_k.shape[0], cache_k.shape[1]
    rows = PAGE * N_HEADS
    ck = cache_k.reshape(depth * n_phys, rows, HEAD_W)
    cv = cache_v.reshape(depth * n_phys, rows, HEAD_W)
    heads3 = lambda lo: zs[:, lo:lo + QK_WIDTH].reshape(Bs, N_HEADS, HEAD_W)
    q3 = heads3(POOL_WIDTH)
    kn3 = heads3(POOL_WIDTH + QK_WIDTH)
    vn3 = heads3(POOL_WIDTH + 2 * QK_WIDTH)

    slopes = 2.0 ** (-8.0 * np.arange(1, N_HEADS + 1) / N_HEADS)
    key_row = np.arange(n_past * N_HEADS)
    key_pos, key_head = key_row // N_HEADS, key_row % N_HEADS
    row_head = np.arange(2 * N_HEADS) % N_HEADS
    alibi = -slopes[row_head][:, None] * (n_past - key_pos)[None, :].astype(np.float64)
    bias = jnp.asarray(np.where(row_head[:, None] == key_head[None, :], alibi, -np.inf), F32)

    def page_spec(p):
        return pl.BlockSpec((None, rows, HEAD_W), lambda b, pt: (layer * n_phys + pt[b, p], 0, 0))

    const2 = lambda b, pt: (0, 0)
    row3 = pl.BlockSpec((None, N_HEADS, HEAD_W), lambda b, pt: (b, 0, 0))
    grid_spec = pltpu.PrefetchScalarGridSpec(
        num_scalar_prefetch=1,
        grid=(Bs,),
        in_specs=[pl.BlockSpec((4, DIFF_HD), const2), row3, row3, row3,
                  pl.BlockSpec((2 * N_HEADS, n_past * N_HEADS), const2)]
                 + [page_spec(p) for p in range(n_pages)]
                 + [page_spec(p) for p in range(n_pages)],
        out_specs=row3,
        scratch_shapes=[pltpu.VMEM((2 * N_HEADS, n_past * N_HEADS), F32)],
    )
    return pl.pallas_call(
        functools.partial(_paged_attn_kernel, n_pages=n_pages, lam0=lam0),
        grid_spec=grid_spec,
        out_shape=jax.ShapeDtypeStruct((Bs, N_HEADS, HEAD_W), F32),
        compiler_params=_params(("arbitrary",)),
        name="paged_diff_attn",
    )(page_table, lamp, q3, kn3, vn3, bias, *([ck] * n_pages), *([cv] * n_pages))


def _cross_sample_kernel(q_ref, k_ref, v_ref, o_ref):
    n_rows = N_MEM * N_CROSS_HEADS * 2
    groups = 2 * N_CROSS_HEADS
    nt_dims = (((1,), (1,)), ((), ()))
    row = lax.broadcasted_iota(jnp.int32, (groups, n_rows), 0)
    lane = lax.broadcasted_iota(jnp.int32, (groups, n_rows), 1)
    own_lane = (lane & (groups - 1)) == row
    head_lane = (lane & (groups - 1)) == (row & (N_CROSS_HEADS - 1))
    for i in range(q_ref.shape[0]):
        s8 = lax.dot_general(q_ref[i], k_ref[i].astype(BF16), nt_dims, preferred_element_type=F32)
        part = jnp.sum(jnp.where(own_lane, s8, 0.0), axis=0, keepdims=True)
        full = part + pltpu.roll(part, n_rows - N_CROSS_HEADS, axis=1)
        s = jnp.where(head_lane, jnp.broadcast_to(full, (groups, n_rows)), NEG_INF)
        m = jnp.max(s, axis=1, keepdims=True)
        p = jnp.exp(s - m)
        a = p / jnp.sum(p, axis=1, keepdims=True)
        w = jnp.where(row < N_CROSS_HEADS, a, pltpu.roll(a, N_CROSS_HEADS, axis=1))
        o_ref[i] = jnp.dot(w.astype(BF16), v_ref[i].astype(BF16),
                           preferred_element_type=F32).astype(o_ref.dtype)


def _cross_sample(qc, mem_k, mem_v, layer):
    Bs = qc.shape[0]
    depth = mem_k.shape[0]
    half = CROSS_HD // 2
    n_rows = N_MEM * N_CROSS_HEADS * 2

    def stored_rows(mem):
        m6 = mem.reshape(depth, Bs, N_MEM, N_CROSS_HEADS, 2, half)
        return jnp.transpose(m6, (0, 1, 2, 4, 3, 5)).reshape(depth * Bs, n_rows, half)

    q8 = jnp.transpose(qc.reshape(Bs, N_CROSS_HEADS, 2, half), (0, 2, 1, 3)).reshape(Bs, 2 * N_CROSS_HEADS, half)
    nb = CROSS_ROWS_PER_STEP
    mem_blk = pl.BlockSpec((nb, n_rows, half), lambda b: (layer * (Bs // nb) + b, 0, 0))
    blk = lambda b: (b, 0, 0)
    o8 = pl.pallas_call(
        _cross_sample_kernel,
        grid=(Bs // nb,),
        in_specs=[pl.BlockSpec((nb, 2 * N_CROSS_HEADS, half), blk), mem_blk, mem_blk],
        out_specs=pl.BlockSpec((nb, 2 * N_CROSS_HEADS, half), blk),
        out_shape=jax.ShapeDtypeStruct((Bs, 2 * N_CROSS_HEADS, half), BF16),
        compiler_params=_params(("arbitrary",)),
        name="cross_sample",
    )(q8, stored_rows(mem_k), stored_rows(mem_v))
    return jnp.transpose(o8.reshape(Bs, 2, N_CROSS_HEADS, half), (0, 2, 1, 3)).reshape(Bs, D_MODEL)


def kernel(x_prompt, x_sample, mem_prompt, cache_k, cache_v, cache_mem_k, cache_mem_v, state_pool, state_conv, page_table, g_mix, w_in, pool_w, pool_scale, lam_q1, lam_k1, lam_q2, lam_k2, subln_g, w_out, g_cross, wq_c, wk_c, wv_c, wo_c, g_ffn, w_up, conv_w, conv_b, w_down, g_final):
    depth = w_in.shape[0]
    Bp, T, D = x_prompt.shape
    Bs = x_sample.shape[0]
    W2 = 2 * D_FF
    bf = lambda a: a.astype(BF16)
    w_in_b, pool_w_b, w_out_b = bf(w_in), bf(pool_w), bf(w_out)
    wq_b, wk_b, wv_b, wo_b, w_up_b, w_down_b = bf(wq_c), bf(wk_c), bf(wv_c), bf(wo_c), bf(w_up), bf(w_down)
    gfin = g_final.reshape(1, D)

    xp = x_prompt
    xs = x_sample.reshape(Bs, D)
    outs = {n: [] for n in ("kp", "vp", "mkp", "mvp", "pp", "cp", "ks", "vs", "ps", "cs")}
    for l in range(depth):
        lam0 = _lambda_init(l)
        last = l == depth - 1
        g_mix_l = g_mix[l].reshape(1, D)
        g_cross_l = g_cross[l].reshape(1, D)
        g_ffn_l = g_ffn[l].reshape(1, D)
        ps_l = pool_scale[l].reshape(1, POOL_WIDTH)
        cb_l = conv_b[l].reshape(1, W2)
        lamp = jnp.stack([lam_q1[l], lam_k1[l], lam_q2[l], lam_k2[l]])

        mk, mv, mkt, mvb = _mem_kv(mem_prompt, wk_b[l], wv_b[l])
        kp, vp, qt, kb, vt, po, tail = _in_proj_prompt(xp, g_mix_l, w_in_b[l], pool_w_b[l], ps_l)
        o = _flash_diff_attn(qt, kb, vt, lamp, subln_g[l].reshape(HEAD_W, 1), lam0)
        x1, qc = _out_q(xp.reshape(Bp * T, D), po.reshape(Bp * T, POOL_WIDTH), o.reshape(Bp * T, -1),
                        subln_g[l].reshape(1, HEAD_W), w_out_b[l], g_cross_l, wq_b[l], ROW_TILE)
        oc = _cross_prompt(qc.reshape(Bp, T, D), mkt, mvb, ROW_TILE)
        xp, up_tail = _ffn(x1.reshape(Bp, T, D), oc, None, wo_b[l], g_ffn_l, w_up_b[l], conv_w[l], cb_l,
                           w_down_b[l], gfin, ROW_TILE, last)
        outs["kp"].append(kp.reshape(Bp, T, N_HEADS, HEAD_W))
        outs["vp"].append(vp.reshape(Bp, T, N_HEADS, HEAD_W))
        outs["mkp"].append(mk.reshape(Bp, N_MEM, N_CROSS_HEADS, CROSS_HD))
        outs["mvp"].append(mv.reshape(Bp, N_MEM, N_CROSS_HEADS, CROSS_HD))
        outs["pp"].append(tail[:, 2 * SUBLANES - POOL_PAST:])
        outs["cp"].append(up_tail[:, SUBLANES - (CONV_W - 1):])

        st_t = jnp.swapaxes(state_pool[l], 0, 1)
        zs, pos, nst = _in_proj_sample(xs, g_mix_l, w_in_b[l], st_t, pool_w_b[l], ps_l)
        o_raw = _paged_attn(zs, page_table, cache_k, cache_v, l, lamp, lam0)
        x1s, qcs = _out_q(xs, pos, o_raw.reshape(Bs, -1), subln_g[l].reshape(1, HEAD_W), w_out_b[l],
                          g_cross_l, wq_b[l], Bs, subln_scale=1.0 - lam0)
        ocs = _cross_sample(qcs, cache_mem_k, cache_mem_v, l)
        prev = (state_conv[l][:, 0].reshape(1, Bs, W2), state_conv[l][:, 1].reshape(1, Bs, W2))
        xs3, up_s = _ffn(x1s.reshape(1, Bs, D), ocs.reshape(1, Bs, D), prev, wo_b[l], g_ffn_l, w_up_b[l],
                         conv_w[l], cb_l, w_down_b[l], gfin, Bs, last)
        xs = xs3.reshape(Bs, D)
        outs["ks"].append(zs[:, POOL_WIDTH + QK_WIDTH:POOL_WIDTH + 2 * QK_WIDTH].reshape(Bs, 1, N_HEADS, HEAD_W))
        outs["vs"].append(zs[:, POOL_WIDTH + 2 * QK_WIDTH:].reshape(Bs, 1, N_HEADS, HEAD_W))
        outs["ps"].append(jnp.swapaxes(nst, 0, 1))
        outs["cs"].append(jnp.stack([state_conv[l][:, 1], up_s.reshape(Bs, W2)], axis=1))

    st = lambda n: jnp.stack(outs[n])
    return (xp, xs.reshape(Bs, 1, D), st("kp"), st("vp"), st("mkp"), st("mvp"), st("pp"), st("cp"),
            st("ks"), st("vs"), st("ps"), st("cs"))
```

```python
import functools
import math

import jax
import jax.numpy as jnp
import numpy as np
from jax import lax
from jax.experimental import pallas as pl
from jax.experimental.pallas import tpu as pltpu

F32 = jnp.float32
BF16 = jnp.bfloat16

D_MODEL = 1024
POOL_WIDTH = 512
POOL_WINDOWS = (2, 4, 8, 16)
POOL_GROUP = 128
POOL_PAST = 15
N_HEADS = 4
HEAD_W = 128
DIFF_HD = 64
QK_WIDTH = 512
IN_WIDTH = 2048
N_MEM = 256
N_CROSS_HEADS = 4
CROSS_HD = 256
D_FF = 2816
CONV_W = 3
EPS = 1e-6
SUBLN_EPS = 1e-5
PAGE = 128

LANES = 128
SUBLANES = 8
VMEM_LIMIT = 56 * 1024 * 1024

ROW_TILE = 512
TQ = 256
TK = 256
FF_CHUNK = 256
CROSS_ROWS_PER_STEP = 4
NEG_INF = float("-inf")


def _lambda_init(layer_idx):
    return 0.8 - 0.6 * math.exp(-0.3 * layer_idx)


def _rms(x, g, eps):
    ms = jnp.mean(x * x, axis=-1, keepdims=True)
    return x * lax.rsqrt(ms + eps) * g


def _params(sem, vmem=VMEM_LIMIT):
    return pltpu.CompilerParams(dimension_semantics=sem, vmem_limit_bytes=vmem)


def _const_spec(shape):
    nd = len(shape)
    return pl.BlockSpec(shape, lambda *_: (0,) * nd)


def _pool_groups(u, shifted, cnt_fn, pw_ref, ps_ref, po_ref):
    for g, w in enumerate(POOL_WINDOWS):
        cols = slice(g * POOL_GROUP, (g + 1) * POOL_GROUP)
        ug = u[:, cols]
        ws = ug
        for j in range(1, w):
            ws = ws + shifted(j, cols)
        p = ws / cnt_fn(w) - ug
        y = jnp.dot(p.astype(BF16), pw_ref[g], preferred_element_type=F32)
        po_ref[:, cols] = (y * ps_ref[:, cols]).astype(po_ref.dtype)


def _in_proj_prompt_kernel(x_ref, g_ref, w_ref, pw_ref, ps_ref,
                           k_ref, v_ref, qt_ref, kb_ref, vt_ref, po_ref, tail_ref, ext_ref, *, tm):
    t = pl.program_id(1)
    halo = 2 * SUBLANES

    @pl.when(t == 0)
    def _():
        ext_ref[0:halo, :] = jnp.zeros((halo, POOL_WIDTH), F32)

    h = _rms(x_ref[...], g_ref[...], EPS).astype(BF16)
    z = jnp.dot(h, w_ref[...], preferred_element_type=F32)
    u = z[:, :POOL_WIDTH]
    scale = DIFF_HD ** -0.5 * LOG2E
    for hh in range(N_HEADS):
        q0 = POOL_WIDTH + hh * HEAD_W
        k0 = POOL_WIDTH + QK_WIDTH + hh * HEAD_W
        v0 = POOL_WIDTH + 2 * QK_WIDTH + hh * HEAD_W
        k_ref[pl.ds(hh, tm, stride=N_HEADS), :] = z[:, k0:k0 + HEAD_W]
        v_ref[pl.ds(hh, tm, stride=N_HEADS), :] = z[:, v0:v0 + HEAD_W]
        for s in range(tm // TQ):
            rows = slice(s * TQ, (s + 1) * TQ)
            qt_ref[hh, s] = (z[rows, q0:q0 + HEAD_W] * scale).T.astype(BF16)
            vt_ref[hh, s] = z[rows, v0:v0 + HEAD_W].T.astype(BF16)
            kb_ref[hh, s] = z[rows, k0:k0 + HEAD_W].astype(BF16)

    ext_ref[halo:halo + tm, :] = u
    pos = t * tm + lax.broadcasted_iota(jnp.int32, (tm, POOL_GROUP), 0)

    def shifted(j, cols):
        return ext_ref[halo - j:halo - j + tm, cols]

    def cnt(w):
        return jnp.minimum(pos + 1, w).astype(F32)

    _pool_groups(u, shifted, cnt, pw_ref, ps_ref, po_ref)
    tail_ref[...] = u[tm - halo:, :]
    ext_ref[0:halo, :] = u[tm - halo:, :]


def _in_proj_prompt(x, g, w_in, pool_w, pool_scale):
    B, T, D = x.shape
    tm = ROW_TILE
    nt = T // tm
    spt = tm // TQ
    row = lambda b, t: (b, t, 0)
    out_shape = (
        jax.ShapeDtypeStruct((B, T * N_HEADS, HEAD_W), F32),
        jax.ShapeDtypeStruct((B, T * N_HEADS, HEAD_W), F32),
        jax.ShapeDtypeStruct((B, N_HEADS, T // TQ, HEAD_W, TQ), BF16),
        jax.ShapeDtypeStruct((B, N_HEADS, T // TK, TK, HEAD_W), BF16),
        jax.ShapeDtypeStruct((B, N_HEADS, T // TK, HEAD_W, TK), BF16),
        jax.ShapeDtypeStruct((B, T, POOL_WIDTH), BF16),
        jax.ShapeDtypeStruct((B, 2 * SUBLANES, POOL_WIDTH), F32),
    )
    tile5 = lambda b, t: (b, 0, t, 0, 0)
    return pl.pallas_call(
        functools.partial(_in_proj_prompt_kernel, tm=tm),
        grid=(B, nt),
        in_specs=[
            pl.BlockSpec((None, tm, D), row),
            _const_spec((1, D)),
            _const_spec((D, IN_WIDTH)),
            _const_spec((len(POOL_WINDOWS), POOL_GROUP, POOL_GROUP)),
            _const_spec((1, POOL_WIDTH)),
        ],
        out_specs=(
            pl.BlockSpec((None, tm * N_HEADS, HEAD_W), row),
            pl.BlockSpec((None, tm * N_HEADS, HEAD_W), row),
            pl.BlockSpec((None, N_HEADS, spt, HEAD_W, TQ), tile5),
            pl.BlockSpec((None, N_HEADS, spt, TK, HEAD_W), tile5),
            pl.BlockSpec((None, N_HEADS, spt, HEAD_W, TK), tile5),
            pl.BlockSpec((None, tm, POOL_WIDTH), row),
            pl.BlockSpec((None, 2 * SUBLANES, POOL_WIDTH), lambda b, t: (b, 0, 0)),
        ),
        out_shape=out_shape,
        scratch_shapes=[pltpu.VMEM((tm + 2 * SUBLANES, POOL_WIDTH), F32)],
        compiler_params=_params(("arbitrary", "arbitrary")),
        name="in_proj_prompt",
    )(x, g, w_in, pool_w, pool_scale)


def _lambda_value(lamp_ref, lam0):
    lp = lamp_ref[...]
    s1 = jnp.sum(lp[0:1] * lp[1:2], axis=1, keepdims=True)
    s2 = jnp.sum(lp[2:3] * lp[3:4], axis=1, keepdims=True)
    return jnp.exp(s1) - jnp.exp(s2) + lam0


def _bf16_terms(x, n):
    terms = []
    for _ in range(n):
        t = float(np.asarray(x, np.float32).astype(jnp.bfloat16).astype(np.float32))
        terms.append(t)
        x -= t
    return terms


LOG2E = math.log2(math.e)
LOG2E_TERMS = _bf16_terms(LOG2E, 3)


def _alibi_operands():
    n = len(LOG2E_TERMS)
    slopes = 2.0 ** (-8.0 * np.arange(1, N_HEADS + 1) / N_HEADS)
    r = (np.arange(2 * TQ) % TQ).astype(np.float64)
    q_side = np.zeros((N_HEADS, HEAD_W, 2 * TQ), np.float64)
    k_side = np.zeros((TK, HEAD_W), np.float64)
    for t, e_t in enumerate(LOG2E_TERMS):
        q_side[:, t, :] = (slopes * e_t)[:, None]
        q_side[:, n + t, :] = slopes[:, None] * r[None, :]
        k_side[:, t] = np.arange(TK)
        k_side[:, n + t] = -e_t
    return jnp.asarray(q_side, BF16), jnp.asarray(k_side, BF16)


def _flash_kernel(lamp_ref, g_ref, aq_ref, ak_ref, qt_ref, k_ref, vt_ref, o_ref,
                  rhs_ref, m_ref, acc_ref, p_ref, alpha_ref, *, lam0):
    i = pl.program_id(1)
    two_tq = 2 * TQ
    slopes = [2.0 ** (-8.0 * (hh + 1) / N_HEADS) for hh in range(N_HEADS)]
    log2e_hi = float(np.float32(LOG2E))
    log2e_lo = LOG2E - log2e_hi

    z64 = jnp.zeros((DIFF_HD, TQ), BF16)
    for hh in range(N_HEADS):
        qt = qt_ref[hh]
        top = jnp.concatenate([qt[:DIFF_HD], z64], axis=1)
        mid = jnp.concatenate([z64, qt[DIFF_HD:]], axis=1)
        rhs_ref[hh] = jnp.concatenate([top, mid, aq_ref[hh]], axis=0)
    m_ref[...] = jnp.full(m_ref.shape, NEG_INF, F32)
    acc_ref[...] = jnp.zeros(acc_ref.shape, F32)
    p_ref[...] = jnp.zeros(p_ref.shape, BF16)
    alpha_ref[...] = jnp.ones(alpha_ref.shape, F32)
    aug_k = ak_ref[...]

    def score_tiles(tiles):
        scores = []
        for hh in range(N_HEADS):
            per_tile = []
            for j, masked in tiles:
                kk = jnp.concatenate([k_ref[hh, j], aug_k], axis=1)
                s = jnp.dot(kk, rhs_ref[hh], preferred_element_type=F32)
                if masked:
                    c_idx = lax.broadcasted_iota(jnp.int32, (TK, two_tq), 0)
                    r_idx = lax.broadcasted_iota(jnp.int32, (TK, two_tq), 1) & (TQ - 1)
                    s = jnp.where(c_idx <= r_idx, s, NEG_INF)
                per_tile.append(s)
            scores.append(per_tile)
        return scores

    def softmax_tiles(scores, tiles):
        probs = []
        for hh in range(N_HEADS):
            dlts = []
            for j, _ in tiles:
                sx = slopes[hh] * ((i - j) * TQ).astype(F32)
                dlts.append(sx * log2e_hi + sx * log2e_lo)
            m_old = m_ref[hh]
            m_new = m_old
            for s, dlt in zip(scores[hh], dlts):
                m_new = jnp.maximum(m_new, jnp.max(s, axis=0, keepdims=True) - dlt)
            parts = [jnp.exp2(s - (m_new + dlt)).astype(BF16) for s, dlt in zip(scores[hh], dlts)]
            p = parts[0] if len(parts) == 1 else jnp.concatenate(parts, axis=0)
            probs.append((p, jnp.exp2(m_old - m_new)))
            m_ref[hh] = m_new
        return probs

    def accumulate(v_tiles, probs):
        ones_rows = jnp.ones((2 * SUBLANES, TK * len(v_tiles)), BF16)
        for hh in range(N_HEADS):
            p, alpha = probs[hh]
            vts = [vt_ref[hh, j] for j in v_tiles]
            vt = vts[0] if len(vts) == 1 else jnp.concatenate(vts, axis=1)
            v1 = jnp.concatenate([vt, ones_rows], axis=0)
            pv = jnp.dot(v1, p, preferred_element_type=F32)
            acc_ref[hh] = alpha * acc_ref[hh] + pv

    def previous_probs():
        return [(p_ref[hh], alpha_ref[hh]) for hh in range(N_HEADS)]

    def body(jj, carry):
        ja = 2 * jj
        tiles = [(ja, False), (ja + 1, False)]
        scores = score_tiles(tiles)
        pa = jnp.maximum(ja - 2, 0)
        accumulate([pa, pa + 1], previous_probs())
        for hh, (p, alpha) in enumerate(softmax_tiles(scores, tiles)):
            p_ref[hh] = p
            alpha_ref[hh] = alpha
        return carry

    n_pairs = lax.shift_right_logical(i, 1)
    lax.fori_loop(0, n_pairs, body, 0)
    last_pair = jnp.maximum(2 * n_pairs - 2, 0)

    def tail(tiles):
        scores = score_tiles(tiles)
        accumulate([last_pair, last_pair + 1], previous_probs())
        accumulate([j for j, _ in tiles], softmax_tiles(scores, tiles))

    odd = (i & 1) == 1

    @pl.when(odd)
    def _():
        tail([(i - 1, False), (i, True)])

    @pl.when(jnp.logical_not(odd))
    def _():
        tail([(i, True)])


    lam = _lambda_value(lamp_ref, lam0)
    for hh in range(N_HEADS):
        acc = acc_ref[hh]
        o1 = acc[:HEAD_W, :TQ] / acc[HEAD_W:HEAD_W + 1, :TQ]
        o2 = acc[:HEAD_W, TQ:] / acc[HEAD_W:HEAD_W + 1, TQ:]
        ot = o1 - lam * o2
        ms = jnp.mean(ot * ot, axis=0, keepdims=True)
        y = ot * lax.rsqrt(ms + SUBLN_EPS) * g_ref[...] * (1.0 - lam0)
        o_ref[:, hh * HEAD_W:(hh + 1) * HEAD_W] = y.T.astype(o_ref.dtype)


def _flash_diff_attn(qt, kb, vt, lamp, subln_g, lam0):
    B, H, nq = qt.shape[0], qt.shape[1], qt.shape[2]
    nk = kb.shape[2]
    T = nq * TQ
    whole = lambda b, i: (b, 0, 0, 0, 0)
    alibi_q, alibi_k = _alibi_operands()
    return pl.pallas_call(
        functools.partial(_flash_kernel, lam0=lam0),
        grid=(B, nq),
        in_specs=[
            _const_spec((4, DIFF_HD)),
            _const_spec((HEAD_W, 1)),
            _const_spec((H, HEAD_W, 2 * TQ)),
            _const_spec((TK, HEAD_W)),
            pl.BlockSpec((None, H, None, HEAD_W, TQ), lambda b, i: (b, 0, i, 0, 0)),
            pl.BlockSpec((None, H, nk, TK, HEAD_W), whole, pipeline_mode=pl.Buffered(1)),
            pl.BlockSpec((None, H, nk, HEAD_W, TK), whole, pipeline_mode=pl.Buffered(1)),
        ],
        out_specs=pl.BlockSpec((None, TQ, H * HEAD_W), lambda b, i: (b, i, 0)),
        out_shape=jax.ShapeDtypeStruct((B, T, H * HEAD_W), BF16),
        scratch_shapes=[
            pltpu.VMEM((H, 2 * HEAD_W, 2 * TQ), BF16),
            pltpu.VMEM((H, 1, 2 * TQ), F32),
            pltpu.VMEM((H, HEAD_W + 2 * SUBLANES, 2 * TQ), F32),
            pltpu.VMEM((H, 2 * TK, 2 * TQ), BF16),
            pltpu.VMEM((H, 1, 2 * TQ), F32),
        ],
        compiler_params=_params(("arbitrary", "arbitrary")),
        name="flash_diff_attn",
    )(lamp, subln_g, alibi_q, alibi_k, qt, kb, vt)


def _out_q_kernel(x_ref, po_ref, o_ref, sg_ref, wout_ref, gc_ref, wq_ref, x1_ref, qc_ref, *, subln_scale):
    o = o_ref[...]
    if subln_scale is not None:
        parts = []
        for hh in range(N_HEADS):
            oh = o[:, hh * HEAD_W:(hh + 1) * HEAD_W]
            parts.append(_rms(oh, sg_ref[...], SUBLN_EPS) * subln_scale)
        o = jnp.concatenate(parts, axis=1)
    mix = jnp.dot(po_ref[...], wout_ref[0:POOL_WIDTH, :], preferred_element_type=F32)
    mix = mix + jnp.dot(o.astype(BF16), wout_ref[POOL_WIDTH:, :], preferred_element_type=F32)
    x1 = x_ref[...] + mix
    x1_ref[...] = x1
    hc = _rms(x1, gc_ref[...], EPS).astype(BF16)
    qc = jnp.dot(hc, wq_ref[...], preferred_element_type=F32) * (CROSS_HD ** -0.5)
    qc_ref[...] = qc.astype(BF16)


def _out_q(x, po, o, subln_g_row, w_out, g_cross, wq_c, tm, subln_scale=None):
    M, D = x.shape
    row = lambda t: (t, 0)
    return pl.pallas_call(
        functools.partial(_out_q_kernel, subln_scale=subln_scale),
        grid=(M // tm,),
        in_specs=[
            pl.BlockSpec((tm, D), row),
            pl.BlockSpec((tm, POOL_WIDTH), row),
            pl.BlockSpec((tm, N_HEADS * HEAD_W), row),
            _const_spec((1, HEAD_W)),
            _const_spec((D, D)),
            _const_spec((1, D)),
            _const_spec((D, D)),
        ],
        out_specs=(pl.BlockSpec((tm, D), row), pl.BlockSpec((tm, D), row)),
        out_shape=(jax.ShapeDtypeStruct((M, D), F32), jax.ShapeDtypeStruct((M, D), BF16)),
        compiler_params=_params(("arbitrary",)),
        name="out_q",
    )(x, po, o, subln_g_row, w_out, g_cross, wq_c)


def _mem_kv_kernel(mem_ref, wk_ref, wv_ref, mk_ref, mv_ref, mkt_ref, mvb_ref):
    m = mem_ref[...].astype(BF16)
    k = jnp.dot(m, wk_ref[...], preferred_element_type=F32)
    v = jnp.dot(m, wv_ref[...], preferred_element_type=F32)
    mk_ref[...] = k
    mv_ref[...] = v
    mkt_ref[...] = k.T.astype(BF16)
    mvb_ref[...] = v.astype(BF16)


def _mem_kv(mem, wk, wv):
    B, N, D = mem.shape
    blk = lambda b: (b, 0, 0)
    return pl.pallas_call(
        _mem_kv_kernel,
        grid=(B,),
        in_specs=[pl.BlockSpec((None, N, D), blk), _const_spec((D, D)), _const_spec((D, D))],
        out_specs=(pl.BlockSpec((None, N, D), blk), pl.BlockSpec((None, N, D), blk),
                   pl.BlockSpec((None, D, N), blk), pl.BlockSpec((None, N, D), blk)),
        out_shape=(jax.ShapeDtypeStruct((B, N, D), F32), jax.ShapeDtypeStruct((B, N, D), F32),
                   jax.ShapeDtypeStruct((B, D, N), BF16), jax.ShapeDtypeStruct((B, N, D), BF16)),
        compiler_params=_params(("arbitrary",)),
        name="mem_kv",
    )(mem, wk, wv)


def _cross_prompt_kernel(q_ref, kt_ref, v_ref, o_ref):
    for hh in range(N_CROSS_HEADS):
        cols = slice(hh * CROSS_HD, (hh + 1) * CROSS_HD)
        s = jnp.dot(q_ref[:, cols], kt_ref[cols, :], preferred_element_type=F32)
        m = jnp.max(s, axis=1, keepdims=True)
        p = jnp.exp(s - m)
        l = jnp.sum(p, axis=1, keepdims=True)
        o = jnp.dot(p.astype(BF16), v_ref[:, cols], preferred_element_type=F32) / l
        o_ref[:, cols] = o.astype(o_ref.dtype)


def _cross_prompt(qc, mkt, mvb, tm):
    B, T, D = qc.shape
    row = lambda b, t: (b, t, 0)
    blk = lambda b, t: (b, 0, 0)
    return pl.pallas_call(
        _cross_prompt_kernel,
        grid=(B, T // tm),
        in_specs=[pl.BlockSpec((None, tm, D), row),
                  pl.BlockSpec((None, D, N_MEM), blk),
                  pl.BlockSpec((None, N_MEM, D), blk)],
        out_specs=pl.BlockSpec((None, tm, D), row),
        out_shape=jax.ShapeDtypeStruct((B, T, D), BF16),
        compiler_params=_params(("arbitrary", "arbitrary")),
        name="cross_prompt",
    )(qc, mkt, mvb)


def _ffn_kernel(*refs, seq_mode, final_norm, tm):
    if seq_mode:
        (x1_ref, oc_ref, wo_ref, g_ref, wup_ref, cw_ref, cb_ref, wdn_ref, gfin_ref,
         out_ref, up_out_ref, x2_ref, h_ref, act_ref, halo_ref) = refs
    else:
        (x1_ref, oc_ref, p2_ref, p1_ref, wo_ref, g_ref, wup_ref, cw_ref, cb_ref, wdn_ref, gfin_ref,
         out_ref, up_out_ref, x2_ref, h_ref, act_ref) = refs
    t = pl.program_id(1)

    if seq_mode:
        @pl.when(t == 0)
        def _():
            halo_ref[...] = jnp.zeros(halo_ref.shape, F32)

    x2 = x1_ref[...] + jnp.dot(oc_ref[...], wo_ref[...], preferred_element_type=F32)
    x2_ref[...] = x2
    h_ref[...] = _rms(x2, g_ref[...], EPS).astype(BF16)

    def up_proj(c0):
        cols_a = slice(c0, c0 + FF_CHUNK)
        cols_g = slice(D_FF + c0, D_FF + c0 + FF_CHUNK)
        return (jnp.dot(h_ref[...], wup_ref[:, cols_a], preferred_element_type=F32),
                jnp.dot(h_ref[...], wup_ref[:, cols_g], preferred_element_type=F32))

    def conv_part(up, col0):
        cols = slice(col0, col0 + FF_CHUNK)
        if seq_mode:
            ext = jnp.concatenate([halo_ref[:, cols], up], axis=0)
            um1 = ext[SUBLANES - 1:SUBLANES - 1 + tm]
            um2 = ext[SUBLANES - 2:SUBLANES - 2 + tm]
            halo_ref[:, cols] = up[tm - SUBLANES:, :]
            up_out_ref[:, cols] = up[tm - SUBLANES:, :]
        else:
            um1 = p1_ref[:, cols]
            um2 = p2_ref[:, cols]
            up_out_ref[:, cols] = up
        c = cb_ref[:, cols] + um2 * cw_ref[0:1, cols]
        c = c + um1 * cw_ref[1:2, cols]
        return c + up * cw_ref[2:3, cols]

    starts = list(range(0, D_FF, FF_CHUNK))
    nxt = up_proj(starts[0])
    for n, c0 in enumerate(starts):
        up_a, up_g = nxt
        if n + 1 < len(starts):
            nxt = up_proj(starts[n + 1])
        a = conv_part(up_a, c0)
        g = conv_part(up_g, D_FF + c0)
        act_ref[:, c0:c0 + FF_CHUNK] = (g * (1.0 / (1.0 + jnp.exp(-g))) * a).astype(BF16)

    x3 = x2_ref[...] + jnp.dot(act_ref[...], wdn_ref[...], preferred_element_type=F32)
    if final_norm:
        x3 = _rms(x3, gfin_ref[...], EPS)
    out_ref[...] = x3


def _ffn(x1, oc, prev, wo, g_ffn, w_up, conv_w, conv_b, w_down, g_final, tm, final_norm):
    B, T, D = x1.shape
    seq_mode = prev is None
    W2 = 2 * D_FF
    row = lambda b, t: (b, t, 0)
    in_specs = [pl.BlockSpec((None, tm, D), row), pl.BlockSpec((None, tm, D), row)]
    args = [x1, oc]
    if not seq_mode:
        in_specs += [pl.BlockSpec((None, tm, W2), row), pl.BlockSpec((None, tm, W2), row)]
        args += list(prev)
    in_specs += [_const_spec((D, D)), _const_spec((1, D)), _const_spec((D, W2)),
                 _const_spec((CONV_W, W2)), _const_spec((1, W2)), _const_spec((D_FF, D)),
                 _const_spec((1, D))]
    args += [wo, g_ffn, w_up, conv_w, conv_b, w_down, g_final]
    scratch = [pltpu.VMEM((tm, D), F32), pltpu.VMEM((tm, D), BF16), pltpu.VMEM((tm, D_FF), BF16)]
    if seq_mode:
        up_rows = SUBLANES
        up_spec = pl.BlockSpec((None, SUBLANES, W2), lambda b, t: (b, 0, 0))
        scratch += [pltpu.VMEM((SUBLANES, W2), F32)]
    else:
        up_rows = T
        up_spec = pl.BlockSpec((None, tm, W2), row)
    return pl.pallas_call(
        functools.partial(_ffn_kernel, seq_mode=seq_mode, final_norm=final_norm, tm=tm),
        grid=(B, T // tm),
        in_specs=in_specs,
        out_specs=(pl.BlockSpec((None, tm, D), row), up_spec),
        out_shape=(jax.ShapeDtypeStruct((B, T, D), F32), jax.ShapeDtypeStruct((B, up_rows, W2), F32)),
        scratch_shapes=scratch,
        compiler_params=_params(("arbitrary", "arbitrary")),
        name="wo_ffn_seq" if seq_mode else "wo_ffn_step",
    )(*args)


def _in_proj_sample_kernel(x_ref, g_ref, w_ref, st_ref, pw_ref, ps_ref, z_ref, po_ref, nst_ref):
    h = _rms(x_ref[...], g_ref[...], EPS).astype(BF16)
    z = jnp.dot(h, w_ref[...], preferred_element_type=F32)
    z_ref[...] = z
    u = z[:, :POOL_WIDTH]

    def shifted(j, cols):
        return st_ref[POOL_PAST - j, :, cols]

    _pool_groups(u, shifted, lambda w: float(w), pw_ref, ps_ref, po_ref)
    for r in range(POOL_PAST - 1):
        nst_ref[r] = st_ref[r + 1]
    nst_ref[POOL_PAST - 1] = u


def _in_proj_sample(x, g, w_in, st_t, pool_w, pool_scale):
    M, D = x.shape
    return pl.pallas_call(
        _in_proj_sample_kernel,
        grid=(1,),
        in_specs=[_const_spec((M, D)), _const_spec((1, D)), _const_spec((D, IN_WIDTH)),
                  _const_spec((POOL_PAST, M, POOL_WIDTH)),
                  _const_spec((len(POOL_WINDOWS), POOL_GROUP, POOL_GROUP)),
                  _const_spec((1, POOL_WIDTH))],
        out_specs=(_const_spec((M, IN_WIDTH)), _const_spec((M, POOL_WIDTH)),
                   _const_spec((POOL_PAST, M, POOL_WIDTH))),
        out_shape=(jax.ShapeDtypeStruct((M, IN_WIDTH), F32),
                   jax.ShapeDtypeStruct((M, POOL_WIDTH), BF16),
                   jax.ShapeDtypeStruct((POOL_PAST, M, POOL_WIDTH), F32)),
        compiler_params=_params(("arbitrary",)),
        name="in_proj_sample",
    )(x, g, w_in, st_t, pool_w, pool_scale)


def _paged_attn_kernel(pt_ref, lamp_ref, q_ref, kn_ref, vn_ref, bias_ref, *rest, n_pages, lam0):
    del pt_ref
    k_refs = rest[:n_pages]
    v_refs = rest[n_pages:2 * n_pages]
    o_ref = rest[2 * n_pages]
    s_ref = rest[2 * n_pages + 1]
    rows = PAGE * N_HEADS
    nt_dims = (((1,), (1,)), ((), ()))

    q4 = q_ref[...] * (DIFF_HD ** -0.5)
    lane = lax.broadcasted_iota(jnp.int32, (N_HEADS, HEAD_W), 1)
    q_all = jnp.concatenate([jnp.where(lane < DIFF_HD, q4, 0.0),
                             jnp.where(lane >= DIFF_HD, q4, 0.0)], axis=0)
    q_all_b = q_all.astype(BF16)
    for p in range(n_pages):
        kp = k_refs[p][...].astype(BF16)
        s_ref[:, p * rows:(p + 1) * rows] = lax.dot_general(q_all_b, kp, nt_dims, preferred_element_type=F32)

    s = s_ref[...] + bias_ref[...]
    kn8 = jnp.concatenate([kn_ref[...], kn_ref[...]], axis=0)
    s_new = jnp.sum(q_all * kn8, axis=1, keepdims=True)
    m = jnp.maximum(jnp.max(s, axis=1, keepdims=True), s_new)
    pr = jnp.exp(s - m)
    pr_new = jnp.exp(s_new - m)
    l = jnp.sum(pr, axis=1, keepdims=True) + pr_new
    lam = _lambda_value(lamp_ref, lam0)
    rowi = lax.broadcasted_iota(jnp.int32, (2 * N_HEADS, 1), 0)
    coef = jnp.where(rowi < N_HEADS, 1.0, -lam) / l
    w = (pr * coef).astype(BF16)

    vn8 = jnp.concatenate([vn_ref[...], vn_ref[...]], axis=0)
    acc = (pr_new * coef) * vn8
    for p in range(n_pages):
        vp = v_refs[p][...].astype(BF16)
        acc = acc + jnp.dot(w[:, p * rows:(p + 1) * rows], vp, preferred_element_type=F32)
    o_ref[...] = acc[:N_HEADS] + acc[N_HEADS:]


def _paged_attn(zs, page_table, cache_k, cache_v, layer, lamp, lam0):
    Bs = zs.shape[0]
    n_pages = page_table.shape[1]
    n_past = n_pages * PAGE
    depth, n_phys = cache_k.shape[0], cache_k.shape[1]
    rows = PAGE * N_HEADS
    ck = cache_k.reshape(depth * n_phys, rows, HEAD_W)
    cv = cache_v.reshape(depth * n_phys, rows, HEAD_W)
    heads3 = lambda lo: zs[:, lo:lo + QK_WIDTH].reshape(Bs, N_HEADS, HEAD_W)
    q3 = heads3(POOL_WIDTH)
    kn3 = heads3(POOL_WIDTH + QK_WIDTH)
    vn3 = heads3(POOL_WIDTH + 2 * QK_WIDTH)

    slopes = 2.0 ** (-8.0 * np.arange(1, N_HEADS + 1) / N_HEADS)
    key_row = np.arange(n_past * N_HEADS)
    key_pos, key_head = key_row // N_HEADS, key_row % N_HEADS
    row_head = np.arange(2 * N_HEADS) % N_HEADS
    alibi = -slopes[row_head][:, None] * (n_past - key_pos)[None, :].astype(np.float64)
    bias = jnp.asarray(np.where(row_head[:, None] == key_head[None, :], alibi, -np.inf), F32)

    def page_spec(p):
        return pl.BlockSpec((None, rows, HEAD_W), lambda b, pt: (layer * n_phys + pt[b, p], 0, 0))

    const2 = lambda b, pt: (0, 0)
    row3 = pl.BlockSpec((None, N_HEADS, HEAD_W), lambda b, pt: (b, 0, 0))
    grid_spec = pltpu.PrefetchScalarGridSpec(
        num_scalar_prefetch=1,
        grid=(Bs,),
        in_specs=[pl.BlockSpec((4, DIFF_HD), const2), row3, row3, row3,
                  pl.BlockSpec((2 * N_HEADS, n_past * N_HEADS), const2)]
                 + [page_spec(p) for p in range(n_pages)]
                 + [page_spec(p) for p in range(n_pages)],
        out_specs=row3,
        scratch_shapes=[pltpu.VMEM((2 * N_HEADS, n_past * N_HEADS), F32)],
    )
    return pl.pallas_call(
        functools.partial(_paged_attn_kernel, n_pages=n_pages, lam0=lam0),
        grid_spec=grid_spec,
        out_shape=jax.ShapeDtypeStruct((Bs, N_HEADS, HEAD_W), F32),
        compiler_params=_params(("arbitrary",)),
        name="paged_diff_attn",
    )(page_table, lamp, q3, kn3, vn3, bias, *([ck] * n_pages), *([cv] * n_pages))


def _cross_sample_kernel(q_ref, k_ref, v_ref, o_ref):
    n_rows = N_MEM * N_CROSS_HEADS * 2
    groups = 2 * N_CROSS_HEADS
    nt_dims = (((1,), (1,)), ((), ()))
    row = lax.broadcasted_iota(jnp.int32, (groups, n_rows), 0)
    lane = lax.broadcasted_iota(jnp.int32, (groups, n_rows), 1)
    own_lane = (lane & (groups - 1)) == row
    head_lane = (lane & (groups - 1)) == (row & (N_CROSS_HEADS - 1))
    n_b = q_ref.shape[0]
    raw = [lax.dot_general(q_ref[i], k_ref[i].astype(BF16), nt_dims, preferred_element_type=F32)
           for i in range(n_b)]
    weights = []
    for s8 in raw:
        part = jnp.sum(jnp.where(own_lane, s8, 0.0), axis=0, keepdims=True)
        full = part + pltpu.roll(part, n_rows - N_CROSS_HEADS, axis=1)
        s = jnp.where(head_lane, jnp.broadcast_to(full, (groups, n_rows)), NEG_INF)
        m = jnp.max(s, axis=1, keepdims=True)
        p = jnp.exp(s - m)
        a = p / jnp.sum(p, axis=1, keepdims=True)
        w = jnp.where(row < N_CROSS_HEADS, a, pltpu.roll(a, N_CROSS_HEADS, axis=1))
        weights.append(w.astype(BF16))
    for i, w in enumerate(weights):
        o_ref[i] = jnp.dot(w, v_ref[i].astype(BF16),
                           preferred_element_type=F32).astype(o_ref.dtype)


def _cross_sample(qc, mem_k, mem_v, layer):
    Bs = qc.shape[0]
    depth = mem_k.shape[0]
    half = CROSS_HD // 2
    n_rows = N_MEM * N_CROSS_HEADS * 2

    def stored_rows(mem):
        m6 = mem.reshape(depth, Bs, N_MEM, N_CROSS_HEADS, 2, half)
        return jnp.transpose(m6, (0, 1, 2, 4, 3, 5)).reshape(depth * Bs, n_rows, half)

    q8 = jnp.transpose(qc.reshape(Bs, N_CROSS_HEADS, 2, half), (0, 2, 1, 3)).reshape(Bs, 2 * N_CROSS_HEADS, half)
    nb = CROSS_ROWS_PER_STEP
    mem_blk = pl.BlockSpec((nb, n_rows, half), lambda b: (layer * (Bs // nb) + b, 0, 0))
    blk = lambda b: (b, 0, 0)
    o8 = pl.pallas_call(
        _cross_sample_kernel,
        grid=(Bs // nb,),
        in_specs=[pl.BlockSpec((nb, 2 * N_CROSS_HEADS, half), blk), mem_blk, mem_blk],
        out_specs=pl.BlockSpec((nb, 2 * N_CROSS_HEADS, half), blk),
        out_shape=jax.ShapeDtypeStruct((Bs, 2 * N_CROSS_HEADS, half), BF16),
        compiler_params=_params(("arbitrary",)),
        name="cross_sample",
    )(q8, stored_rows(mem_k), stored_rows(mem_v))
    return jnp.transpose(o8.reshape(Bs, 2, N_CROSS_HEADS, half), (0, 2, 1, 3)).reshape(Bs, D_MODEL)


def kernel(x_prompt, x_sample, mem_prompt, cache_k, cache_v, cache_mem_k, cache_mem_v, state_pool, state_conv, page_table, g_mix, w_in, pool_w, pool_scale, lam_q1, lam_k1, lam_q2, lam_k2, subln_g, w_out, g_cross, wq_c, wk_c, wv_c, wo_c, g_ffn, w_up, conv_w, conv_b, w_down, g_final):
    depth = w_in.shape[0]
    Bp, T, D = x_prompt.shape
    Bs = x_sample.shape[0]
    W2 = 2 * D_FF
    bf = lambda a: a.astype(BF16)
    w_in_b, pool_w_b, w_out_b = bf(w_in), bf(pool_w), bf(w_out)
    wq_b, wk_b, wv_b, wo_b, w_up_b, w_down_b = bf(wq_c), bf(wk_c), bf(wv_c), bf(wo_c), bf(w_up), bf(w_down)
    gfin = g_final.reshape(1, D)

    xp = x_prompt
    xs = x_sample.reshape(Bs, D)
    outs = {n: [] for n in ("kp", "vp", "mkp", "mvp", "pp", "cp", "ks", "vs", "ps", "cs")}
    for l in range(depth):
        lam0 = _lambda_init(l)
        last = l == depth - 1
        g_mix_l = g_mix[l].reshape(1, D)
        g_cross_l = g_cross[l].reshape(1, D)
        g_ffn_l = g_ffn[l].reshape(1, D)
        ps_l = pool_scale[l].reshape(1, POOL_WIDTH)
        cb_l = conv_b[l].reshape(1, W2)
        lamp = jnp.stack([lam_q1[l], lam_k1[l], lam_q2[l], lam_k2[l]])

        mk, mv, mkt, mvb = _mem_kv(mem_prompt, wk_b[l], wv_b[l])
        kp, vp, qt, kb, vt, po, tail = _in_proj_prompt(xp, g_mix_l, w_in_b[l], pool_w_b[l], ps_l)
        o = _flash_diff_attn(qt, kb, vt, lamp, subln_g[l].reshape(HEAD_W, 1), lam0)
        x1, qc = _out_q(xp.reshape(Bp * T, D), po.reshape(Bp * T, POOL_WIDTH), o.reshape(Bp * T, -1),
                        subln_g[l].reshape(1, HEAD_W), w_out_b[l], g_cross_l, wq_b[l], ROW_TILE)
        oc = _cross_prompt(qc.reshape(Bp, T, D), mkt, mvb, ROW_TILE)
        xp, up_tail = _ffn(x1.reshape(Bp, T, D), oc, None, wo_b[l], g_ffn_l, w_up_b[l], conv_w[l], cb_l,
                           w_down_b[l], gfin, ROW_TILE, last)
        outs["kp"].append(kp.reshape(Bp, T, N_HEADS, HEAD_W))
        outs["vp"].append(vp.reshape(Bp, T, N_HEADS, HEAD_W))
        outs["mkp"].append(mk.reshape(Bp, N_MEM, N_CROSS_HEADS, CROSS_HD))
        outs["mvp"].append(mv.reshape(Bp, N_MEM, N_CROSS_HEADS, CROSS_HD))
        outs["pp"].append(tail[:, 2 * SUBLANES - POOL_PAST:])
        outs["cp"].append(up_tail[:, SUBLANES - (CONV_W - 1):])

        st_t = jnp.swapaxes(state_pool[l], 0, 1)
        zs, pos, nst = _in_proj_sample(xs, g_mix_l, w_in_b[l], st_t, pool_w_b[l], ps_l)
        o_raw = _paged_attn(zs, page_table, cache_k, cache_v, l, lamp, lam0)
        x1s, qcs = _out_q(xs, pos, o_raw.reshape(Bs, -1), subln_g[l].reshape(1, HEAD_W), w_out_b[l],
                          g_cross_l, wq_b[l], Bs, subln_scale=1.0 - lam0)
        ocs = _cross_sample(qcs, cache_mem_k, cache_mem_v, l)
        prev = (state_conv[l][:, 0].reshape(1, Bs, W2), state_conv[l][:, 1].reshape(1, Bs, W2))
        xs3, up_s = _ffn(x1s.reshape(1, Bs, D), ocs.reshape(1, Bs, D), prev, wo_b[l], g_ffn_l, w_up_b[l],
                         conv_w[l], cb_l, w_down_b[l], gfin, Bs, last)
        xs = xs3.reshape(Bs, D)
        outs["ks"].append(zs[:, POOL_WIDTH + QK_WIDTH:POOL_WIDTH + 2 * QK_WIDTH].reshape(Bs, 1, N_HEADS, HEAD_W))
        outs["vs"].append(zs[:, POOL_WIDTH + 2 * QK_WIDTH:].reshape(Bs, 1, N_HEADS, HEAD_W))
        outs["ps"].append(jnp.swapaxes(nst, 0, 1))
        outs["cs"].append(jnp.stack([state_conv[l][:, 1], up_s.reshape(Bs, W2)], axis=1))

    st = lambda n: jnp.stack(outs[n])
    return (xp, xs.reshape(Bs, 1, D), st("kp"), st("vp"), st("mkp"), st("mvp"), st("pp"), st("cp"),
            st("ks"), st("vs"), st("ps"), st("cs"))
```

```python
import functools
import math

import jax
import jax.numpy as jnp
import numpy as np
from jax import lax
from jax.experimental import pallas as pl
from jax.experimental.pallas import tpu as pltpu

F32 = jnp.float32
BF16 = jnp.bfloat16

D_MODEL = 1024
POOL_WIDTH = 512
POOL_WINDOWS = (2, 4, 8, 16)
POOL_GROUP = 128
POOL_PAST = 15
N_HEADS = 4
HEAD_W = 128
DIFF_HD = 64
QK_WIDTH = 512
IN_WIDTH = 2048
N_MEM = 256
N_CROSS_HEADS = 4
CROSS_HD = 256
D_FF = 2816
CONV_W = 3
EPS = 1e-6
SUBLN_EPS = 1e-5
PAGE = 128

LANES = 128
SUBLANES = 8
VMEM_LIMIT = 56 * 1024 * 1024

ROW_TILE = 512
TQ = 256
TK = 256
FF_CHUNK = 256
CROSS_ROWS_PER_STEP = 4
NEG_INF = float("-inf")


def _lambda_init(layer_idx):
    return 0.8 - 0.6 * math.exp(-0.3 * layer_idx)


def _rms(x, g, eps):
    ms = jnp.mean(x * x, axis=-1, keepdims=True)
    return x * lax.rsqrt(ms + eps) * g


def _params(sem, vmem=VMEM_LIMIT):
    return pltpu.CompilerParams(dimension_semantics=sem, vmem_limit_bytes=vmem)


def _const_spec(shape):
    nd = len(shape)
    return pl.BlockSpec(shape, lambda *_: (0,) * nd)


def _pool_groups(u, shifted, cnt_fn, pw_ref, ps_ref, po_ref):
    for g, w in enumerate(POOL_WINDOWS):
        cols = slice(g * POOL_GROUP, (g + 1) * POOL_GROUP)
        ug = u[:, cols]
        ws = ug
        for j in range(1, w):
            ws = ws + shifted(j, cols)
        p = ws / cnt_fn(w) - ug
        y = jnp.dot(p.astype(BF16), pw_ref[g], preferred_element_type=F32)
        po_ref[:, cols] = (y * ps_ref[:, cols]).astype(po_ref.dtype)


def _in_proj_prompt_kernel(x_ref, g_ref, w_ref, pw_ref, ps_ref,
                           k_ref, v_ref, qt_ref, kb_ref, vt_ref, po_ref, tail_ref, ext_ref, *, tm):
    t = pl.program_id(1)
    halo = 2 * SUBLANES

    @pl.when(t == 0)
    def _():
        ext_ref[0:halo, :] = jnp.zeros((halo, POOL_WIDTH), F32)

    h = _rms(x_ref[...], g_ref[...], EPS).astype(BF16)
    z = jnp.dot(h, w_ref[...], preferred_element_type=F32)
    u = z[:, :POOL_WIDTH]
    scale = DIFF_HD ** -0.5 * LOG2E
    for hh in range(N_HEADS):
        q0 = POOL_WIDTH + hh * HEAD_W
        k0 = POOL_WIDTH + QK_WIDTH + hh * HEAD_W
        v0 = POOL_WIDTH + 2 * QK_WIDTH + hh * HEAD_W
        k_ref[pl.ds(hh, tm, stride=N_HEADS), :] = z[:, k0:k0 + HEAD_W]
        v_ref[pl.ds(hh, tm, stride=N_HEADS), :] = z[:, v0:v0 + HEAD_W]
        for s in range(tm // TQ):
            rows = slice(s * TQ, (s + 1) * TQ)
            qt_ref[hh, s] = (z[rows, q0:q0 + HEAD_W] * scale).T.astype(BF16)
            vt_ref[hh, s] = z[rows, v0:v0 + HEAD_W].T.astype(BF16)
            kb_ref[hh, s] = z[rows, k0:k0 + HEAD_W].astype(BF16)

    ext_ref[halo:halo + tm, :] = u
    pos = t * tm + lax.broadcasted_iota(jnp.int32, (tm, POOL_GROUP), 0)

    def shifted(j, cols):
        return ext_ref[halo - j:halo - j + tm, cols]

    def cnt(w):
        return jnp.minimum(pos + 1, w).astype(F32)

    _pool_groups(u, shifted, cnt, pw_ref, ps_ref, po_ref)
    tail_ref[...] = u[tm - halo:, :]
    ext_ref[0:halo, :] = u[tm - halo:, :]


def _in_proj_prompt(x, g, w_in, pool_w, pool_scale):
    B, T, D = x.shape
    tm = ROW_TILE
    nt = T // tm
    spt = tm // TQ
    row = lambda b, t: (b, t, 0)
    out_shape = (
        jax.ShapeDtypeStruct((B, T * N_HEADS, HEAD_W), F32),
        jax.ShapeDtypeStruct((B, T * N_HEADS, HEAD_W), F32),
        jax.ShapeDtypeStruct((B, N_HEADS, T // TQ, HEAD_W, TQ), BF16),
        jax.ShapeDtypeStruct((B, N_HEADS, T // TK, TK, HEAD_W), BF16),
        jax.ShapeDtypeStruct((B, N_HEADS, T // TK, HEAD_W, TK), BF16),
        jax.ShapeDtypeStruct((B, T, POOL_WIDTH), BF16),
        jax.ShapeDtypeStruct((B, 2 * SUBLANES, POOL_WIDTH), F32),
    )
    tile5 = lambda b, t: (b, 0, t, 0, 0)
    return pl.pallas_call(
        functools.partial(_in_proj_prompt_kernel, tm=tm),
        grid=(B, nt),
        in_specs=[
            pl.BlockSpec((None, tm, D), row),
            _const_spec((1, D)),
            _const_spec((D, IN_WIDTH)),
            _const_spec((len(POOL_WINDOWS), POOL_GROUP, POOL_GROUP)),
            _const_spec((1, POOL_WIDTH)),
        ],
        out_specs=(
            pl.BlockSpec((None, tm * N_HEADS, HEAD_W), row),
            pl.BlockSpec((None, tm * N_HEADS, HEAD_W), row),
            pl.BlockSpec((None, N_HEADS, spt, HEAD_W, TQ), tile5),
            pl.BlockSpec((None, N_HEADS, spt, TK, HEAD_W), tile5),
            pl.BlockSpec((None, N_HEADS, spt, HEAD_W, TK), tile5),
            pl.BlockSpec((None, tm, POOL_WIDTH), row),
            pl.BlockSpec((None, 2 * SUBLANES, POOL_WIDTH), lambda b, t: (b, 0, 0)),
        ),
        out_shape=out_shape,
        scratch_shapes=[pltpu.VMEM((tm + 2 * SUBLANES, POOL_WIDTH), F32)],
        compiler_params=_params(("arbitrary", "arbitrary")),
        name="in_proj_prompt",
    )(x, g, w_in, pool_w, pool_scale)


def _lambda_value(lamp_ref, lam0):
    lp = lamp_ref[...]
    s1 = jnp.sum(lp[0:1] * lp[1:2], axis=1, keepdims=True)
    s2 = jnp.sum(lp[2:3] * lp[3:4], axis=1, keepdims=True)
    return jnp.exp(s1) - jnp.exp(s2) + lam0


def _bf16_terms(x, n):
    terms = []
    for _ in range(n):
        t = float(np.asarray(x, np.float32).astype(jnp.bfloat16).astype(np.float32))
        terms.append(t)
        x -= t
    return terms


LOG2E = math.log2(math.e)
LOG2E_TERMS = _bf16_terms(LOG2E, 3)


def _alibi_operands():
    n = len(LOG2E_TERMS)
    slopes = 2.0 ** (-8.0 * np.arange(1, N_HEADS + 1) / N_HEADS)
    r = (np.arange(2 * TQ) % TQ).astype(np.float64)
    q_side = np.zeros((N_HEADS, HEAD_W, 2 * TQ), np.float64)
    k_side = np.zeros((TK, HEAD_W), np.float64)
    for t, e_t in enumerate(LOG2E_TERMS):
        q_side[:, t, :] = (slopes * e_t)[:, None]
        q_side[:, n + t, :] = slopes[:, None] * r[None, :]
        k_side[:, t] = np.arange(TK)
        k_side[:, n + t] = -e_t
    return jnp.asarray(q_side, BF16), jnp.asarray(k_side, BF16)


def _flash_kernel(lamp_ref, g_ref, aq_ref, ak_ref, qt_ref, k_ref, vt_ref, o_ref,
                  rhs_ref, m_ref, acc_ref, p_ref, alpha_ref, *, lam0):
    i = pl.program_id(1)
    two_tq = 2 * TQ
    slopes = [2.0 ** (-8.0 * (hh + 1) / N_HEADS) for hh in range(N_HEADS)]
    log2e_hi = float(np.float32(LOG2E))
    log2e_lo = LOG2E - log2e_hi

    z64 = jnp.zeros((DIFF_HD, TQ), BF16)
    for hh in range(N_HEADS):
        qt = qt_ref[hh]
        top = jnp.concatenate([qt[:DIFF_HD], z64], axis=1)
        mid = jnp.concatenate([z64, qt[DIFF_HD:]], axis=1)
        rhs_ref[hh] = jnp.concatenate([top, mid, aq_ref[hh]], axis=0)
    m_ref[...] = jnp.full(m_ref.shape, NEG_INF, F32)
    acc_ref[...] = jnp.zeros(acc_ref.shape, F32)
    p_ref[...] = jnp.zeros(p_ref.shape, BF16)
    alpha_ref[...] = jnp.ones(alpha_ref.shape, F32)
    aug_k = ak_ref[...]

    def score_tiles(tiles):
        scores = []
        for hh in range(N_HEADS):
            per_tile = []
            for j, masked in tiles:
                kk = jnp.concatenate([k_ref[hh, j], aug_k], axis=1)
                s = jnp.dot(kk, rhs_ref[hh], preferred_element_type=F32)
                if masked:
                    c_idx = lax.broadcasted_iota(jnp.int32, (TK, two_tq), 0)
                    r_idx = lax.broadcasted_iota(jnp.int32, (TK, two_tq), 1) & (TQ - 1)
                    s = jnp.where(c_idx <= r_idx, s, NEG_INF)
                per_tile.append(s)
            scores.append(per_tile)
        return scores

    def softmax_tiles(scores, tiles):
        probs = []
        for hh in range(N_HEADS):
            dlts = []
            for j, _ in tiles:
                sx = slopes[hh] * ((i - j) * TQ).astype(F32)
                dlts.append(sx * log2e_hi + sx * log2e_lo)
            m_old = m_ref[hh]
            m_new = m_old
            for s, dlt in zip(scores[hh], dlts):
                m_new = jnp.maximum(m_new, jnp.max(s, axis=0, keepdims=True) - dlt)
            parts = [jnp.exp2(s - (m_new + dlt)).astype(BF16) for s, dlt in zip(scores[hh], dlts)]
            p = parts[0] if len(parts) == 1 else jnp.concatenate(parts, axis=0)
            probs.append((p, jnp.exp2(m_old - m_new)))
            m_ref[hh] = m_new
        return probs

    def accumulate(v_tiles, probs):
        ones_rows = jnp.ones((2 * SUBLANES, TK * len(v_tiles)), BF16)
        for hh in range(N_HEADS):
            p, alpha = probs[hh]
            vts = [vt_ref[hh, j] for j in v_tiles]
            vt = vts[0] if len(vts) == 1 else jnp.concatenate(vts, axis=1)
            v1 = jnp.concatenate([vt, ones_rows], axis=0)
            pv = jnp.dot(v1, p, preferred_element_type=F32)
            acc_ref[hh] = alpha * acc_ref[hh] + pv

    def previous_probs():
        return [(p_ref[hh], alpha_ref[hh]) for hh in range(N_HEADS)]

    def body(jj, carry):
        ja = 2 * jj
        tiles = [(ja, False), (ja + 1, False)]
        scores = score_tiles(tiles)
        pa = jnp.maximum(ja - 2, 0)
        accumulate([pa, pa + 1], previous_probs())
        for hh, (p, alpha) in enumerate(softmax_tiles(scores, tiles)):
            p_ref[hh] = p
            alpha_ref[hh] = alpha
        return carry

    n_pairs = lax.shift_right_logical(i, 1)
    lax.fori_loop(0, n_pairs, body, 0)
    last_pair = jnp.maximum(2 * n_pairs - 2, 0)

    def tail(tiles):
        scores = score_tiles(tiles)
        accumulate([last_pair, last_pair + 1], previous_probs())
        accumulate([j for j, _ in tiles], softmax_tiles(scores, tiles))

    odd = (i & 1) == 1

    @pl.when(odd)
    def _():
        tail([(i - 1, False), (i, True)])

    @pl.when(jnp.logical_not(odd))
    def _():
        tail([(i, True)])


    lam = _lambda_value(lamp_ref, lam0)
    for hh in range(N_HEADS):
        acc = acc_ref[hh]
        o1 = acc[:HEAD_W, :TQ] / acc[HEAD_W:HEAD_W + 1, :TQ]
        o2 = acc[:HEAD_W, TQ:] / acc[HEAD_W:HEAD_W + 1, TQ:]
        ot = o1 - lam * o2
        ms = jnp.mean(ot * ot, axis=0, keepdims=True)
        y = ot * lax.rsqrt(ms + SUBLN_EPS) * g_ref[...] * (1.0 - lam0)
        o_ref[:, hh * HEAD_W:(hh + 1) * HEAD_W] = y.T.astype(o_ref.dtype)


def _flash_diff_attn(qt, kb, vt, lamp, subln_g, lam0):
    B, H, nq = qt.shape[0], qt.shape[1], qt.shape[2]
    nk = kb.shape[2]
    T = nq * TQ
    whole = lambda b, i: (b, 0, 0, 0, 0)
    alibi_q, alibi_k = _alibi_operands()
    return pl.pallas_call(
        functools.partial(_flash_kernel, lam0=lam0),
        grid=(B, nq),
        in_specs=[
            _const_spec((4, DIFF_HD)),
            _const_spec((HEAD_W, 1)),
            _const_spec((H, HEAD_W, 2 * TQ)),
            _const_spec((TK, HEAD_W)),
            pl.BlockSpec((None, H, None, HEAD_W, TQ), lambda b, i: (b, 0, i, 0, 0)),
            pl.BlockSpec((None, H, nk, TK, HEAD_W), whole, pipeline_mode=pl.Buffered(1)),
            pl.BlockSpec((None, H, nk, HEAD_W, TK), whole, pipeline_mode=pl.Buffered(1)),
        ],
        out_specs=pl.BlockSpec((None, TQ, H * HEAD_W), lambda b, i: (b, i, 0)),
        out_shape=jax.ShapeDtypeStruct((B, T, H * HEAD_W), BF16),
        scratch_shapes=[
            pltpu.VMEM((H, 2 * HEAD_W, 2 * TQ), BF16),
            pltpu.VMEM((H, 1, 2 * TQ), F32),
            pltpu.VMEM((H, HEAD_W + 2 * SUBLANES, 2 * TQ), F32),
            pltpu.VMEM((H, 2 * TK, 2 * TQ), BF16),
            pltpu.VMEM((H, 1, 2 * TQ), F32),
        ],
        compiler_params=_params(("arbitrary", "arbitrary")),
        name="flash_diff_attn",
    )(lamp, subln_g, alibi_q, alibi_k, qt, kb, vt)


def _out_q_kernel(*refs, subln_scale, shared_memory):
    if shared_memory:
        x_ref, po_ref, o_ref, sg_ref, wout_ref, gc_ref, wq_ref, kt_ref, v_ref, x1_ref, out2_ref = refs
    else:
        x_ref, po_ref, o_ref, sg_ref, wout_ref, gc_ref, wq_ref, x1_ref, out2_ref = refs
    o = o_ref[...]
    if subln_scale is not None:
        parts = []
        for hh in range(N_HEADS):
            oh = o[:, hh * HEAD_W:(hh + 1) * HEAD_W]
            parts.append(_rms(oh, sg_ref[...], SUBLN_EPS) * subln_scale)
        o = jnp.concatenate(parts, axis=1)
    mix = jnp.dot(po_ref[...], wout_ref[0:POOL_WIDTH, :], preferred_element_type=F32)
    mix = mix + jnp.dot(o.astype(BF16), wout_ref[POOL_WIDTH:, :], preferred_element_type=F32)
    x1 = x_ref[...] + mix
    x1_ref[...] = x1
    hc = _rms(x1, gc_ref[...], EPS).astype(BF16)
    qc = (jnp.dot(hc, wq_ref[...], preferred_element_type=F32) * (CROSS_HD ** -0.5)).astype(BF16)
    if not shared_memory:
        out2_ref[...] = qc
        return
    for hh in range(N_CROSS_HEADS):
        cols = slice(hh * CROSS_HD, (hh + 1) * CROSS_HD)
        s = jnp.dot(qc[:, cols], kt_ref[cols, :], preferred_element_type=F32)
        m = jnp.max(s, axis=1, keepdims=True)
        p = jnp.exp(s - m)
        l = jnp.sum(p, axis=1, keepdims=True)
        oc = jnp.dot(p.astype(BF16), v_ref[:, cols], preferred_element_type=F32) / l
        out2_ref[:, cols] = oc.astype(out2_ref.dtype)


def _out_q(x, po, o, subln_g_row, w_out, g_cross, wq_c, tm, subln_scale=None, memory=None):
    B, T, D = x.shape
    row = lambda b, t: (b, t, 0)
    blk = lambda b, t: (b, 0, 0)
    in_specs = [
        pl.BlockSpec((None, tm, D), row),
        pl.BlockSpec((None, tm, POOL_WIDTH), row),
        pl.BlockSpec((None, tm, N_HEADS * HEAD_W), row),
        _const_spec((1, HEAD_W)),
        _const_spec((D, D)),
        _const_spec((1, D)),
        _const_spec((D, D)),
    ]
    args = [x, po, o, subln_g_row, w_out, g_cross, wq_c]
    if memory is not None:
        in_specs += [pl.BlockSpec((None, D, N_MEM), blk), pl.BlockSpec((None, N_MEM, D), blk)]
        args += list(memory)
    return pl.pallas_call(
        functools.partial(_out_q_kernel, subln_scale=subln_scale, shared_memory=memory is not None),
        grid=(B, T // tm),
        in_specs=in_specs,
        out_specs=(pl.BlockSpec((None, tm, D), row), pl.BlockSpec((None, tm, D), row)),
        out_shape=(jax.ShapeDtypeStruct((B, T, D), F32), jax.ShapeDtypeStruct((B, T, D), BF16)),
        compiler_params=_params(("arbitrary", "arbitrary")),
        name="out_cross" if memory is not None else "out_q",
    )(*args)


def _mem_kv_kernel(mem_ref, wk_ref, wv_ref, mk_ref, mv_ref, mkt_ref, mvb_ref):
    m = mem_ref[...].astype(BF16)
    k = jnp.dot(m, wk_ref[...], preferred_element_type=F32)
    v = jnp.dot(m, wv_ref[...], preferred_element_type=F32)
    mk_ref[...] = k
    mv_ref[...] = v
    mkt_ref[...] = k.T.astype(BF16)
    mvb_ref[...] = v.astype(BF16)


def _mem_kv(mem, wk, wv):
    B, N, D = mem.shape
    blk = lambda b: (b, 0, 0)
    return pl.pallas_call(
        _mem_kv_kernel,
        grid=(B,),
        in_specs=[pl.BlockSpec((None, N, D), blk), _const_spec((D, D)), _const_spec((D, D))],
        out_specs=(pl.BlockSpec((None, N, D), blk), pl.BlockSpec((None, N, D), blk),
                   pl.BlockSpec((None, D, N), blk), pl.BlockSpec((None, N, D), blk)),
        out_shape=(jax.ShapeDtypeStruct((B, N, D), F32), jax.ShapeDtypeStruct((B, N, D), F32),
                   jax.ShapeDtypeStruct((B, D, N), BF16), jax.ShapeDtypeStruct((B, N, D), BF16)),
        compiler_params=_params(("arbitrary",)),
        name="mem_kv",
    )(mem, wk, wv)


def _ffn_kernel(*refs, seq_mode, final_norm, tm):
    if seq_mode:
        (x1_ref, oc_ref, wo_ref, g_ref, wup_ref, cw_ref, cb_ref, wdn_ref, gfin_ref,
         out_ref, up_out_ref, x2_ref, h_ref, act_ref, halo_ref) = refs
    else:
        (x1_ref, oc_ref, p2_ref, p1_ref, wo_ref, g_ref, wup_ref, cw_ref, cb_ref, wdn_ref, gfin_ref,
         out_ref, up_out_ref, x2_ref, h_ref, act_ref) = refs
    t = pl.program_id(1)

    if seq_mode:
        @pl.when(t == 0)
        def _():
            halo_ref[...] = jnp.zeros(halo_ref.shape, F32)

    x2 = x1_ref[...] + jnp.dot(oc_ref[...], wo_ref[...], preferred_element_type=F32)
    x2_ref[...] = x2
    h_ref[...] = _rms(x2, g_ref[...], EPS).astype(BF16)

    def up_proj(c0):
        cols_a = slice(c0, c0 + FF_CHUNK)
        cols_g = slice(D_FF + c0, D_FF + c0 + FF_CHUNK)
        return (jnp.dot(h_ref[...], wup_ref[:, cols_a], preferred_element_type=F32),
                jnp.dot(h_ref[...], wup_ref[:, cols_g], preferred_element_type=F32))

    def conv_part(up, col0):
        cols = slice(col0, col0 + FF_CHUNK)
        if seq_mode:
            ext = jnp.concatenate([halo_ref[:, cols], up], axis=0)
            um1 = ext[SUBLANES - 1:SUBLANES - 1 + tm]
            um2 = ext[SUBLANES - 2:SUBLANES - 2 + tm]
            halo_ref[:, cols] = up[tm - SUBLANES:, :]
            up_out_ref[:, cols] = up[tm - SUBLANES:, :]
        else:
            um1 = p1_ref[:, cols]
            um2 = p2_ref[:, cols]
            up_out_ref[:, cols] = up
        c = cb_ref[:, cols] + um2 * cw_ref[0:1, cols]
        c = c + um1 * cw_ref[1:2, cols]
        return c + up * cw_ref[2:3, cols]

    starts = list(range(0, D_FF, FF_CHUNK))
    nxt = up_proj(starts[0])
    for n, c0 in enumerate(starts):
        up_a, up_g = nxt
        if n + 1 < len(starts):
            nxt = up_proj(starts[n + 1])
        a = conv_part(up_a, c0)
        g = conv_part(up_g, D_FF + c0)
        act_ref[:, c0:c0 + FF_CHUNK] = (g * (1.0 / (1.0 + jnp.exp(-g))) * a).astype(BF16)

    x3 = x2_ref[...] + jnp.dot(act_ref[...], wdn_ref[...], preferred_element_type=F32)
    if final_norm:
        x3 = _rms(x3, gfin_ref[...], EPS)
    out_ref[...] = x3


def _ffn(x1, oc, prev, wo, g_ffn, w_up, conv_w, conv_b, w_down, g_final, tm, final_norm):
    B, T, D = x1.shape
    seq_mode = prev is None
    W2 = 2 * D_FF
    row = lambda b, t: (b, t, 0)
    in_specs = [pl.BlockSpec((None, tm, D), row), pl.BlockSpec((None, tm, D), row)]
    args = [x1, oc]
    if not seq_mode:
        in_specs += [pl.BlockSpec((None, tm, W2), row), pl.BlockSpec((None, tm, W2), row)]
        args += list(prev)
    in_specs += [_const_spec((D, D)), _const_spec((1, D)), _const_spec((D, W2)),
                 _const_spec((CONV_W, W2)), _const_spec((1, W2)), _const_spec((D_FF, D)),
                 _const_spec((1, D))]
    args += [wo, g_ffn, w_up, conv_w, conv_b, w_down, g_final]
    scratch = [pltpu.VMEM((tm, D), F32), pltpu.VMEM((tm, D), BF16), pltpu.VMEM((tm, D_FF), BF16)]
    if seq_mode:
        up_rows = SUBLANES
        up_spec = pl.BlockSpec((None, SUBLANES, W2), lambda b, t: (b, 0, 0))
        scratch += [pltpu.VMEM((SUBLANES, W2), F32)]
    else:
        up_rows = T
        up_spec = pl.BlockSpec((None, tm, W2), row)
    return pl.pallas_call(
        functools.partial(_ffn_kernel, seq_mode=seq_mode, final_norm=final_norm, tm=tm),
        grid=(B, T // tm),
        in_specs=in_specs,
        out_specs=(pl.BlockSpec((None, tm, D), row), up_spec),
        out_shape=(jax.ShapeDtypeStruct((B, T, D), F32), jax.ShapeDtypeStruct((B, up_rows, W2), F32)),
        scratch_shapes=scratch,
        compiler_params=_params(("arbitrary", "arbitrary")),
        name="wo_ffn_seq" if seq_mode else "wo_ffn_step",
    )(*args)


def _in_proj_sample_kernel(x_ref, g_ref, w_ref, st_ref, pw_ref, ps_ref, z_ref, po_ref, nst_ref):
    h = _rms(x_ref[...], g_ref[...], EPS).astype(BF16)
    z = jnp.dot(h, w_ref[...], preferred_element_type=F32)
    z_ref[...] = z
    u = z[:, :POOL_WIDTH]

    def shifted(j, cols):
        return st_ref[POOL_PAST - j, :, cols]

    _pool_groups(u, shifted, lambda w: float(w), pw_ref, ps_ref, po_ref)
    for r in range(POOL_PAST - 1):
        nst_ref[r] = st_ref[r + 1]
    nst_ref[POOL_PAST - 1] = u


def _in_proj_sample(x, g, w_in, st_t, pool_w, pool_scale):
    M, D = x.shape
    return pl.pallas_call(
        _in_proj_sample_kernel,
        grid=(1,),
        in_specs=[_const_spec((M, D)), _const_spec((1, D)), _const_spec((D, IN_WIDTH)),
                  _const_spec((POOL_PAST, M, POOL_WIDTH)),
                  _const_spec((len(POOL_WINDOWS), POOL_GROUP, POOL_GROUP)),
                  _const_spec((1, POOL_WIDTH))],
        out_specs=(_const_spec((M, IN_WIDTH)), _const_spec((M, POOL_WIDTH)),
                   _const_spec((POOL_PAST, M, POOL_WIDTH))),
        out_shape=(jax.ShapeDtypeStruct((M, IN_WIDTH), F32),
                   jax.ShapeDtypeStruct((M, POOL_WIDTH), BF16),
                   jax.ShapeDtypeStruct((POOL_PAST, M, POOL_WIDTH), F32)),
        compiler_params=_params(("arbitrary",)),
        name="in_proj_sample",
    )(x, g, w_in, st_t, pool_w, pool_scale)


def _paged_attn_kernel(pt_ref, lamp_ref, q_ref, kn_ref, vn_ref, bias_ref, *rest, n_pages, lam0):
    del pt_ref
    k_refs = rest[:n_pages]
    v_refs = rest[n_pages:2 * n_pages]
    o_ref = rest[2 * n_pages]
    s_ref = rest[2 * n_pages + 1]
    rows = PAGE * N_HEADS
    nt_dims = (((1,), (1,)), ((), ()))

    q4 = q_ref[...] * (DIFF_HD ** -0.5)
    lane = lax.broadcasted_iota(jnp.int32, (N_HEADS, HEAD_W), 1)
    q_all = jnp.concatenate([jnp.where(lane < DIFF_HD, q4, 0.0),
                             jnp.where(lane >= DIFF_HD, q4, 0.0)], axis=0)
    q_all_b = q_all.astype(BF16)
    for p in range(n_pages):
        kp = k_refs[p][...].astype(BF16)
        s_ref[:, p * rows:(p + 1) * rows] = lax.dot_general(q_all_b, kp, nt_dims, preferred_element_type=F32)

    s = s_ref[...] + bias_ref[...]
    kn8 = jnp.concatenate([kn_ref[...], kn_ref[...]], axis=0)
    s_new = jnp.sum(q_all * kn8, axis=1, keepdims=True)
    m = jnp.maximum(jnp.max(s, axis=1, keepdims=True), s_new)
    pr = jnp.exp(s - m)
    pr_new = jnp.exp(s_new - m)
    l = jnp.sum(pr, axis=1, keepdims=True) + pr_new
    lam = _lambda_value(lamp_ref, lam0)
    rowi = lax.broadcasted_iota(jnp.int32, (2 * N_HEADS, 1), 0)
    coef = jnp.where(rowi < N_HEADS, 1.0, -lam) / l
    w = (pr * coef).astype(BF16)

    vn8 = jnp.concatenate([vn_ref[...], vn_ref[...]], axis=0)
    acc = (pr_new * coef) * vn8
    for p in range(n_pages):
        vp = v_refs[p][...].astype(BF16)
        acc = acc + jnp.dot(w[:, p * rows:(p + 1) * rows], vp, preferred_element_type=F32)
    o_ref[...] = acc[:N_HEADS] + acc[N_HEADS:]


def _paged_attn(zs, page_table, cache_k, cache_v, layer, lamp, lam0):
    Bs = zs.shape[0]
    n_pages = page_table.shape[1]
    n_past = n_pages * PAGE
    depth, n_phys = cache_k.shape[0], cache_k.shape[1]
    rows = PAGE * N_HEADS
    ck = cache_k.reshape(depth * n_phys, rows, HEAD_W)
    cv = cache_v.reshape(depth * n_phys, rows, HEAD_W)
    heads3 = lambda lo: zs[:, lo:lo + QK_WIDTH].reshape(Bs, N_HEADS, HEAD_W)
    q3 = heads3(POOL_WIDTH)
    kn3 = heads3(POOL_WIDTH + QK_WIDTH)
    vn3 = heads3(POOL_WIDTH + 2 * QK_WIDTH)

    slopes = 2.0 ** (-8.0 * np.arange(1, N_HEADS + 1) / N_HEADS)
    key_row = np.arange(n_past * N_HEADS)
    key_pos, key_head = key_row // N_HEADS, key_row % N_HEADS
    row_head = np.arange(2 * N_HEADS) % N_HEADS
    alibi = -slopes[row_head][:, None] * (n_past - key_pos)[None, :].astype(np.float64)
    bias = jnp.asarray(np.where(row_head[:, None] == key_head[None, :], alibi, -np.inf), F32)

    def page_spec(p):
        return pl.BlockSpec((None, rows, HEAD_W), lambda b, pt: (layer * n_phys + pt[b, p], 0, 0))

    const2 = lambda b, pt: (0, 0)
    row3 = pl.BlockSpec((None, N_HEADS, HEAD_W), lambda b, pt: (b, 0, 0))
    grid_spec = pltpu.PrefetchScalarGridSpec(
        num_scalar_prefetch=1,
        grid=(Bs,),
        in_specs=[pl.BlockSpec((4, DIFF_HD), const2), row3, row3, row3,
                  pl.BlockSpec((2 * N_HEADS, n_past * N_HEADS), const2)]
                 + [page_spec(p) for p in range(n_pages)]
                 + [page_spec(p) for p in range(n_pages)],
        out_specs=row3,
        scratch_shapes=[pltpu.VMEM((2 * N_HEADS, n_past * N_HEADS), F32)],
    )
    return pl.pallas_call(
        functools.partial(_paged_attn_kernel, n_pages=n_pages, lam0=lam0),
        grid_spec=grid_spec,
        out_shape=jax.ShapeDtypeStruct((Bs, N_HEADS, HEAD_W), F32),
        compiler_params=_params(("arbitrary",)),
        name="paged_diff_attn",
    )(page_table, lamp, q3, kn3, vn3, bias, *([ck] * n_pages), *([cv] * n_pages))


def _cross_sample_kernel(q_ref, k_ref, v_ref, o_ref):
    n_rows = N_MEM * N_CROSS_HEADS * 2
    groups = 2 * N_CROSS_HEADS
    nt_dims = (((1,), (1,)), ((), ()))
    row = lax.broadcasted_iota(jnp.int32, (groups, n_rows), 0)
    lane = lax.broadcasted_iota(jnp.int32, (groups, n_rows), 1)
    own_lane = (lane & (groups - 1)) == row
    head_lane = (lane & (groups - 1)) == (row & (N_CROSS_HEADS - 1))
    n_b = q_ref.shape[0]
    raw = [lax.dot_general(q_ref[i], k_ref[i].astype(BF16), nt_dims, preferred_element_type=F32)
           for i in range(n_b)]
    weights = []
    for s8 in raw:
        part = jnp.sum(jnp.where(own_lane, s8, 0.0), axis=0, keepdims=True)
        full = part + pltpu.roll(part, n_rows - N_CROSS_HEADS, axis=1)
        s = jnp.where(head_lane, jnp.broadcast_to(full, (groups, n_rows)), NEG_INF)
        m = jnp.max(s, axis=1, keepdims=True)
        p = jnp.exp(s - m)
        a = p / jnp.sum(p, axis=1, keepdims=True)
        w = jnp.where(row < N_CROSS_HEADS, a, pltpu.roll(a, N_CROSS_HEADS, axis=1))
        weights.append(w.astype(BF16))
    for i, w in enumerate(weights):
        o_ref[i] = jnp.dot(w, v_ref[i].astype(BF16),
                           preferred_element_type=F32).astype(o_ref.dtype)


def _cross_sample(qc, mem_k, mem_v, layer):
    Bs = qc.shape[0]
    depth = mem_k.shape[0]
    half = CROSS_HD // 2
    n_rows = N_MEM * N_CROSS_HEADS * 2

    def stored_rows(mem):
        m6 = mem.reshape(depth, Bs, N_MEM, N_CROSS_HEADS, 2, half)
        return jnp.transpose(m6, (0, 1, 2, 4, 3, 5)).reshape(depth * Bs, n_rows, half)

    q8 = jnp.transpose(qc.reshape(Bs, N_CROSS_HEADS, 2, half), (0, 2, 1, 3)).reshape(Bs, 2 * N_CROSS_HEADS, half)
    nb = CROSS_ROWS_PER_STEP
    mem_blk = pl.BlockSpec((nb, n_rows, half), lambda b: (layer * (Bs // nb) + b, 0, 0))
    blk = lambda b: (b, 0, 0)
    o8 = pl.pallas_call(
        _cross_sample_kernel,
        grid=(Bs // nb,),
        in_specs=[pl.BlockSpec((nb, 2 * N_CROSS_HEADS, half), blk), mem_blk, mem_blk],
        out_specs=pl.BlockSpec((nb, 2 * N_CROSS_HEADS, half), blk),
        out_shape=jax.ShapeDtypeStruct((Bs, 2 * N_CROSS_HEADS, half), BF16),
        compiler_params=_params(("arbitrary",)),
        name="cross_sample",
    )(q8, stored_rows(mem_k), stored_rows(mem_v))
    return jnp.transpose(o8.reshape(Bs, 2, N_CROSS_HEADS, half), (0, 2, 1, 3)).reshape(Bs, D_MODEL)


def kernel(x_prompt, x_sample, mem_prompt, cache_k, cache_v, cache_mem_k, cache_mem_v, state_pool, state_conv, page_table, g_mix, w_in, pool_w, pool_scale, lam_q1, lam_k1, lam_q2, lam_k2, subln_g, w_out, g_cross, wq_c, wk_c, wv_c, wo_c, g_ffn, w_up, conv_w, conv_b, w_down, g_final):
    depth = w_in.shape[0]
    Bp, T, D = x_prompt.shape
    Bs = x_sample.shape[0]
    W2 = 2 * D_FF
    bf = lambda a: a.astype(BF16)
    w_in_b, pool_w_b, w_out_b = bf(w_in), bf(pool_w), bf(w_out)
    wq_b, wk_b, wv_b, wo_b, w_up_b, w_down_b = bf(wq_c), bf(wk_c), bf(wv_c), bf(wo_c), bf(w_up), bf(w_down)
    gfin = g_final.reshape(1, D)

    xp = x_prompt
    xs = x_sample.reshape(Bs, D)
    outs = {n: [] for n in ("kp", "vp", "mkp", "mvp", "pp", "cp", "ks", "vs", "ps", "cs")}
    for l in range(depth):
        lam0 = _lambda_init(l)
        last = l == depth - 1
        g_mix_l = g_mix[l].reshape(1, D)
        g_cross_l = g_cross[l].reshape(1, D)
        g_ffn_l = g_ffn[l].reshape(1, D)
        ps_l = pool_scale[l].reshape(1, POOL_WIDTH)
        cb_l = conv_b[l].reshape(1, W2)
        lamp = jnp.stack([lam_q1[l], lam_k1[l], lam_q2[l], lam_k2[l]])

        mk, mv, mkt, mvb = _mem_kv(mem_prompt, wk_b[l], wv_b[l])
        kp, vp, qt, kb, vt, po, tail = _in_proj_prompt(xp, g_mix_l, w_in_b[l], pool_w_b[l], ps_l)
        o = _flash_diff_attn(qt, kb, vt, lamp, subln_g[l].reshape(HEAD_W, 1), lam0)
        x1, oc = _out_q(xp, po, o, subln_g[l].reshape(1, HEAD_W), w_out_b[l], g_cross_l, wq_b[l], ROW_TILE,
                        memory=(mkt, mvb))
        xp, up_tail = _ffn(x1, oc, None, wo_b[l], g_ffn_l, w_up_b[l], conv_w[l], cb_l,
                           w_down_b[l], gfin, ROW_TILE, last)
        outs["kp"].append(kp.reshape(Bp, T, N_HEADS, HEAD_W))
        outs["vp"].append(vp.reshape(Bp, T, N_HEADS, HEAD_W))
        outs["mkp"].append(mk.reshape(Bp, N_MEM, N_CROSS_HEADS, CROSS_HD))
        outs["mvp"].append(mv.reshape(Bp, N_MEM, N_CROSS_HEADS, CROSS_HD))
        outs["pp"].append(tail[:, 2 * SUBLANES - POOL_PAST:])
        outs["cp"].append(up_tail[:, SUBLANES - (CONV_W - 1):])

        st_t = jnp.swapaxes(state_pool[l], 0, 1)
        zs, pos, nst = _in_proj_sample(xs, g_mix_l, w_in_b[l], st_t, pool_w_b[l], ps_l)
        o_raw = _paged_attn(zs, page_table, cache_k, cache_v, l, lamp, lam0)
        x1s, qcs = _out_q(xs.reshape(1, Bs, D), pos.reshape(1, Bs, -1), o_raw.reshape(1, Bs, -1),
                          subln_g[l].reshape(1, HEAD_W), w_out_b[l], g_cross_l, wq_b[l], Bs,
                          subln_scale=1.0 - lam0)
        ocs = _cross_sample(qcs.reshape(Bs, D), cache_mem_k, cache_mem_v, l)
        prev = (state_conv[l][:, 0].reshape(1, Bs, W2), state_conv[l][:, 1].reshape(1, Bs, W2))
        xs3, up_s = _ffn(x1s, ocs.reshape(1, Bs, D), prev, wo_b[l], g_ffn_l, w_up_b[l],
                         conv_w[l], cb_l, w_down_b[l], gfin, Bs, last)
        xs = xs3.reshape(Bs, D)
        outs["ks"].append(zs[:, POOL_WIDTH + QK_WIDTH:POOL_WIDTH + 2 * QK_WIDTH].reshape(Bs, 1, N_HEADS, HEAD_W))
        outs["vs"].append(zs[:, POOL_WIDTH + 2 * QK_WIDTH:].reshape(Bs, 1, N_HEADS, HEAD_W))
        outs["ps"].append(jnp.swapaxes(nst, 0, 1))
        outs["cs"].append(jnp.stack([state_conv[l][:, 1], up_s.reshape(Bs, W2)], axis=1))

    st = lambda n: jnp.stack(outs[n])
    return (xp, xs.reshape(Bs, 1, D), st("kp"), st("vp"), st("mkp"), st("mvp"), st("pp"), st("cp"),
            st("ks"), st("vs"), st("ps"), st("cs"))
```

```python
import functools
import math

import jax
import jax.numpy as jnp
import numpy as np
from jax import lax
from jax.experimental import pallas as pl
from jax.experimental.pallas import tpu as pltpu

F32 = jnp.float32
BF16 = jnp.bfloat16

D_MODEL = 1024
POOL_WIDTH = 512
POOL_WINDOWS = (2, 4, 8, 16)
POOL_GROUP = 128
POOL_PAST = 15
N_HEADS = 4
HEAD_W = 128
DIFF_HD = 64
QK_WIDTH = 512
IN_WIDTH = 2048
N_MEM = 256
N_CROSS_HEADS = 4
CROSS_HD = 256
D_FF = 2816
CONV_W = 3
EPS = 1e-6
SUBLN_EPS = 1e-5
PAGE = 128

LANES = 128
SUBLANES = 8
VMEM_LIMIT = 56 * 1024 * 1024

ROW_TILE = 512
PROJ_TILE = 1024
TQ = 256
TK = 256
FF_CHUNK = 256
CROSS_ROWS_PER_STEP = 4
NEG_INF = float("-inf")


def _lambda_init(layer_idx):
    return 0.8 - 0.6 * math.exp(-0.3 * layer_idx)


def _rms(x, g, eps):
    ms = jnp.mean(x * x, axis=-1, keepdims=True)
    return x * lax.rsqrt(ms + eps) * g


def _params(sem, vmem=VMEM_LIMIT):
    return pltpu.CompilerParams(dimension_semantics=sem, vmem_limit_bytes=vmem)


def _const_spec(shape):
    nd = len(shape)
    return pl.BlockSpec(shape, lambda *_: (0,) * nd)


def _pool_groups(u, shifted, cnt_fn, pw_ref, ps_ref, po_ref):
    for g, w in enumerate(POOL_WINDOWS):
        cols = slice(g * POOL_GROUP, (g + 1) * POOL_GROUP)
        ug = u[:, cols]
        ws = ug
        for j in range(1, w):
            ws = ws + shifted(j, cols)
        p = ws / cnt_fn(w) - ug
        y = jnp.dot(p.astype(BF16), pw_ref[g], preferred_element_type=F32)
        po_ref[:, cols] = (y * ps_ref[:, cols]).astype(po_ref.dtype)


def _in_proj_prompt_kernel(x_ref, g_ref, w_ref, pw_ref, ps_ref,
                           k_ref, v_ref, qt_ref, kb_ref, vt_ref, po_ref, tail_ref, ext_ref, *, tm):
    t = pl.program_id(1)
    halo = 2 * SUBLANES

    @pl.when(t == 0)
    def _():
        ext_ref[0:halo, :] = jnp.zeros((halo, POOL_WIDTH), F32)

    h = _rms(x_ref[...], g_ref[...], EPS).astype(BF16)
    z = jnp.dot(h, w_ref[...], preferred_element_type=F32)
    u = z[:, :POOL_WIDTH]
    scale = DIFF_HD ** -0.5 * LOG2E
    for hh in range(N_HEADS):
        q0 = POOL_WIDTH + hh * HEAD_W
        k0 = POOL_WIDTH + QK_WIDTH + hh * HEAD_W
        v0 = POOL_WIDTH + 2 * QK_WIDTH + hh * HEAD_W
        k_ref[pl.ds(hh, tm, stride=N_HEADS), :] = z[:, k0:k0 + HEAD_W]
        v_ref[pl.ds(hh, tm, stride=N_HEADS), :] = z[:, v0:v0 + HEAD_W]
        for s in range(tm // TQ):
            rows = slice(s * TQ, (s + 1) * TQ)
            qt_ref[hh, s] = (z[rows, q0:q0 + HEAD_W] * scale).T.astype(BF16)
            vt_ref[hh, s] = z[rows, v0:v0 + HEAD_W].T.astype(BF16)
            kb_ref[hh, s] = z[rows, k0:k0 + HEAD_W].astype(BF16)

    ext_ref[halo:halo + tm, :] = u
    pos = t * tm + lax.broadcasted_iota(jnp.int32, (tm, POOL_GROUP), 0)

    def shifted(j, cols):
        return ext_ref[halo - j:halo - j + tm, cols]

    def cnt(w):
        return jnp.minimum(pos + 1, w).astype(F32)

    _pool_groups(u, shifted, cnt, pw_ref, ps_ref, po_ref)
    tail_ref[...] = u[tm - halo:, :]
    ext_ref[0:halo, :] = u[tm - halo:, :]


def _in_proj_prompt(x, g, w_in, pool_w, pool_scale):
    B, T, D = x.shape
    tm = PROJ_TILE
    nt = T // tm
    spt = tm // TQ
    row = lambda b, t: (b, t, 0)
    out_shape = (
        jax.ShapeDtypeStruct((B, T * N_HEADS, HEAD_W), F32),
        jax.ShapeDtypeStruct((B, T * N_HEADS, HEAD_W), F32),
        jax.ShapeDtypeStruct((B, N_HEADS, T // TQ, HEAD_W, TQ), BF16),
        jax.ShapeDtypeStruct((B, N_HEADS, T // TK, TK, HEAD_W), BF16),
        jax.ShapeDtypeStruct((B, N_HEADS, T // TK, HEAD_W, TK), BF16),
        jax.ShapeDtypeStruct((B, T, POOL_WIDTH), BF16),
        jax.ShapeDtypeStruct((B, 2 * SUBLANES, POOL_WIDTH), F32),
    )
    tile5 = lambda b, t: (b, 0, t, 0, 0)
    return pl.pallas_call(
        functools.partial(_in_proj_prompt_kernel, tm=tm),
        grid=(B, nt),
        in_specs=[
            pl.BlockSpec((None, tm, D), row),
            _const_spec((1, D)),
            _const_spec((D, IN_WIDTH)),
            _const_spec((len(POOL_WINDOWS), POOL_GROUP, POOL_GROUP)),
            _const_spec((1, POOL_WIDTH)),
        ],
        out_specs=(
            pl.BlockSpec((None, tm * N_HEADS, HEAD_W), row),
            pl.BlockSpec((None, tm * N_HEADS, HEAD_W), row),
            pl.BlockSpec((None, N_HEADS, spt, HEAD_W, TQ), tile5),
            pl.BlockSpec((None, N_HEADS, spt, TK, HEAD_W), tile5),
            pl.BlockSpec((None, N_HEADS, spt, HEAD_W, TK), tile5),
            pl.BlockSpec((None, tm, POOL_WIDTH), row),
            pl.BlockSpec((None, 2 * SUBLANES, POOL_WIDTH), lambda b, t: (b, 0, 0)),
        ),
        out_shape=out_shape,
        scratch_shapes=[pltpu.VMEM((tm + 2 * SUBLANES, POOL_WIDTH), F32)],
        compiler_params=_params(("arbitrary", "arbitrary")),
        name="in_proj_prompt",
    )(x, g, w_in, pool_w, pool_scale)


def _lambda_value(lamp_ref, lam0):
    lp = lamp_ref[...]
    s1 = jnp.sum(lp[0:1] * lp[1:2], axis=1, keepdims=True)
    s2 = jnp.sum(lp[2:3] * lp[3:4], axis=1, keepdims=True)
    return jnp.exp(s1) - jnp.exp(s2) + lam0


def _bf16_terms(x, n):
    terms = []
    for _ in range(n):
        t = float(np.asarray(x, np.float32).astype(jnp.bfloat16).astype(np.float32))
        terms.append(t)
        x -= t
    return terms


LOG2E = math.log2(math.e)
LOG2E_TERMS = _bf16_terms(LOG2E, 3)


def _alibi_operands():
    n = len(LOG2E_TERMS)
    slopes = 2.0 ** (-8.0 * np.arange(1, N_HEADS + 1) / N_HEADS)
    r = (np.arange(2 * TQ) % TQ).astype(np.float64)
    q_side = np.zeros((N_HEADS, HEAD_W, 2 * TQ), np.float64)
    k_side = np.zeros((TK, HEAD_W), np.float64)
    for t, e_t in enumerate(LOG2E_TERMS):
        q_side[:, t, :] = (slopes * e_t)[:, None]
        q_side[:, n + t, :] = slopes[:, None] * r[None, :]
        k_side[:, t] = np.arange(TK)
        k_side[:, n + t] = -e_t
    return jnp.asarray(q_side, BF16), jnp.asarray(k_side, BF16)


def _flash_kernel(lamp_ref, g_ref, aq_ref, ak_ref, qt_ref, k_ref, vt_ref, o_ref,
                  rhs_ref, m_ref, acc_ref, p_ref, alpha_ref, *, lam0):
    i = pl.program_id(1)
    two_tq = 2 * TQ
    slopes = [2.0 ** (-8.0 * (hh + 1) / N_HEADS) for hh in range(N_HEADS)]
    log2e_hi = float(np.float32(LOG2E))
    log2e_lo = LOG2E - log2e_hi

    z64 = jnp.zeros((DIFF_HD, TQ), BF16)
    for hh in range(N_HEADS):
        qt = qt_ref[hh]
        top = jnp.concatenate([qt[:DIFF_HD], z64], axis=1)
        mid = jnp.concatenate([z64, qt[DIFF_HD:]], axis=1)
        rhs_ref[hh] = jnp.concatenate([top, mid, aq_ref[hh]], axis=0)
    m_ref[...] = jnp.full(m_ref.shape, NEG_INF, F32)
    acc_ref[...] = jnp.zeros(acc_ref.shape, F32)
    p_ref[...] = jnp.zeros(p_ref.shape, BF16)
    alpha_ref[...] = jnp.ones(alpha_ref.shape, F32)
    aug_k = ak_ref[...]

    def score_tiles(tiles):
        scores = []
        for hh in range(N_HEADS):
            per_tile = []
            for j, masked in tiles:
                kk = jnp.concatenate([k_ref[hh, j], aug_k], axis=1)
                s = jnp.dot(kk, rhs_ref[hh], preferred_element_type=F32)
                if masked:
                    c_idx = lax.broadcasted_iota(jnp.int32, (TK, two_tq), 0)
                    r_idx = lax.broadcasted_iota(jnp.int32, (TK, two_tq), 1) & (TQ - 1)
                    s = jnp.where(c_idx <= r_idx, s, NEG_INF)
                per_tile.append(s)
            scores.append(per_tile)
        return scores

    def softmax_tiles(scores, tiles):
        probs = []
        for hh in range(N_HEADS):
            dlts = []
            for j, _ in tiles:
                sx = slopes[hh] * ((i - j) * TQ).astype(F32)
                dlts.append(sx * log2e_hi + sx * log2e_lo)
            m_old = m_ref[hh]
            m_new = m_old
            for s, dlt in zip(scores[hh], dlts):
                m_new = jnp.maximum(m_new, jnp.max(s, axis=0, keepdims=True) - dlt)
            parts = [jnp.exp2(s - (m_new + dlt)).astype(BF16) for s, dlt in zip(scores[hh], dlts)]
            p = parts[0] if len(parts) == 1 else jnp.concatenate(parts, axis=0)
            probs.append((p, jnp.exp2(m_old - m_new)))
            m_ref[hh] = m_new
        return probs

    def accumulate(v_tiles, probs):
        ones_rows = jnp.ones((2 * SUBLANES, TK * len(v_tiles)), BF16)
        for hh in range(N_HEADS):
            p, alpha = probs[hh]
            vts = [vt_ref[hh, j] for j in v_tiles]
            vt = vts[0] if len(vts) == 1 else jnp.concatenate(vts, axis=1)
            v1 = jnp.concatenate([vt, ones_rows], axis=0)
            pv = jnp.dot(v1, p, preferred_element_type=F32)
            acc_ref[hh] = alpha * acc_ref[hh] + pv

    def previous_probs():
        return [(p_ref[hh], alpha_ref[hh]) for hh in range(N_HEADS)]

    def body(jj, carry):
        ja = 2 * jj
        tiles = [(ja, False), (ja + 1, False)]
        scores = score_tiles(tiles)
        pa = jnp.maximum(ja - 2, 0)
        accumulate([pa, pa + 1], previous_probs())
        for hh, (p, alpha) in enumerate(softmax_tiles(scores, tiles)):
            p_ref[hh] = p
            alpha_ref[hh] = alpha
        return carry

    n_pairs = lax.shift_right_logical(i, 1)
    lax.fori_loop(0, n_pairs, body, 0)
    last_pair = jnp.maximum(2 * n_pairs - 2, 0)

    def tail(tiles):
        scores = score_tiles(tiles)
        accumulate([last_pair, last_pair + 1], previous_probs())
        accumulate([j for j, _ in tiles], softmax_tiles(scores, tiles))

    odd = (i & 1) == 1

    @pl.when(odd)
    def _():
        tail([(i - 1, False), (i, True)])

    @pl.when(jnp.logical_not(odd))
    def _():
        tail([(i, True)])


    lam = _lambda_value(lamp_ref, lam0)
    for hh in range(N_HEADS):
        acc = acc_ref[hh]
        o1 = acc[:HEAD_W, :TQ] / acc[HEAD_W:HEAD_W + 1, :TQ]
        o2 = acc[:HEAD_W, TQ:] / acc[HEAD_W:HEAD_W + 1, TQ:]
        ot = o1 - lam * o2
        ms = jnp.mean(ot * ot, axis=0, keepdims=True)
        y = ot * lax.rsqrt(ms + SUBLN_EPS) * g_ref[...] * (1.0 - lam0)
        o_ref[:, hh * HEAD_W:(hh + 1) * HEAD_W] = y.T.astype(o_ref.dtype)


def _flash_diff_attn(qt, kb, vt, lamp, subln_g, lam0):
    B, H, nq = qt.shape[0], qt.shape[1], qt.shape[2]
    nk = kb.shape[2]
    T = nq * TQ
    whole = lambda b, i: (b, 0, 0, 0, 0)
    alibi_q, alibi_k = _alibi_operands()
    return pl.pallas_call(
        functools.partial(_flash_kernel, lam0=lam0),
        grid=(B, nq),
        in_specs=[
            _const_spec((4, DIFF_HD)),
            _const_spec((HEAD_W, 1)),
            _const_spec((H, HEAD_W, 2 * TQ)),
            _const_spec((TK, HEAD_W)),
            pl.BlockSpec((None, H, None, HEAD_W, TQ), lambda b, i: (b, 0, i, 0, 0)),
            pl.BlockSpec((None, H, nk, TK, HEAD_W), whole, pipeline_mode=pl.Buffered(1)),
            pl.BlockSpec((None, H, nk, HEAD_W, TK), whole, pipeline_mode=pl.Buffered(1)),
        ],
        out_specs=pl.BlockSpec((None, TQ, H * HEAD_W), lambda b, i: (b, i, 0)),
        out_shape=jax.ShapeDtypeStruct((B, T, H * HEAD_W), BF16),
        scratch_shapes=[
            pltpu.VMEM((H, 2 * HEAD_W, 2 * TQ), BF16),
            pltpu.VMEM((H, 1, 2 * TQ), F32),
            pltpu.VMEM((H, HEAD_W + 2 * SUBLANES, 2 * TQ), F32),
            pltpu.VMEM((H, 2 * TK, 2 * TQ), BF16),
            pltpu.VMEM((H, 1, 2 * TQ), F32),
        ],
        compiler_params=_params(("arbitrary", "arbitrary")),
        name="flash_diff_attn",
    )(lamp, subln_g, alibi_q, alibi_k, qt, kb, vt)


def _out_q_kernel(*refs, subln_scale, shared_memory):
    if shared_memory:
        x_ref, po_ref, o_ref, sg_ref, wout_ref, gc_ref, wq_ref, kt_ref, v_ref, x1_ref, out2_ref = refs
    else:
        x_ref, po_ref, o_ref, sg_ref, wout_ref, gc_ref, wq_ref, x1_ref, out2_ref = refs
    o = o_ref[...]
    if subln_scale is not None:
        parts = []
        for hh in range(N_HEADS):
            oh = o[:, hh * HEAD_W:(hh + 1) * HEAD_W]
            parts.append(_rms(oh, sg_ref[...], SUBLN_EPS) * subln_scale)
        o = jnp.concatenate(parts, axis=1)
    mix = jnp.dot(po_ref[...], wout_ref[0:POOL_WIDTH, :], preferred_element_type=F32)
    mix = mix + jnp.dot(o.astype(BF16), wout_ref[POOL_WIDTH:, :], preferred_element_type=F32)
    x1 = x_ref[...] + mix
    x1_ref[...] = x1
    hc = _rms(x1, gc_ref[...], EPS).astype(BF16)
    qc = (jnp.dot(hc, wq_ref[...], preferred_element_type=F32) * (CROSS_HD ** -0.5)).astype(BF16)
    if not shared_memory:
        out2_ref[...] = qc
        return
    for hh in range(N_CROSS_HEADS):
        cols = slice(hh * CROSS_HD, (hh + 1) * CROSS_HD)
        s = jnp.dot(qc[:, cols], kt_ref[cols, :], preferred_element_type=F32)
        m = jnp.max(s, axis=1, keepdims=True)
        p = jnp.exp(s - m)
        l = jnp.sum(p, axis=1, keepdims=True)
        oc = jnp.dot(p.astype(BF16), v_ref[:, cols], preferred_element_type=F32) / l
        out2_ref[:, cols] = oc.astype(out2_ref.dtype)


def _out_q(x, po, o, subln_g_row, w_out, g_cross, wq_c, tm, subln_scale=None, memory=None):
    B, T, D = x.shape
    row = lambda b, t: (b, t, 0)
    blk = lambda b, t: (b, 0, 0)
    in_specs = [
        pl.BlockSpec((None, tm, D), row),
        pl.BlockSpec((None, tm, POOL_WIDTH), row),
        pl.BlockSpec((None, tm, N_HEADS * HEAD_W), row),
        _const_spec((1, HEAD_W)),
        _const_spec((D, D)),
        _const_spec((1, D)),
        _const_spec((D, D)),
    ]
    args = [x, po, o, subln_g_row, w_out, g_cross, wq_c]
    if memory is not None:
        in_specs += [pl.BlockSpec((None, D, N_MEM), blk), pl.BlockSpec((None, N_MEM, D), blk)]
        args += list(memory)
    return pl.pallas_call(
        functools.partial(_out_q_kernel, subln_scale=subln_scale, shared_memory=memory is not None),
        grid=(B, T // tm),
        in_specs=in_specs,
        out_specs=(pl.BlockSpec((None, tm, D), row), pl.BlockSpec((None, tm, D), row)),
        out_shape=(jax.ShapeDtypeStruct((B, T, D), F32), jax.ShapeDtypeStruct((B, T, D), BF16)),
        compiler_params=_params(("arbitrary", "arbitrary")),
        name="out_cross" if memory is not None else "out_q",
    )(*args)


def _mem_kv_kernel(mem_ref, wk_ref, wv_ref, mk_ref, mv_ref, mkt_ref, mvb_ref):
    m = mem_ref[...].astype(BF16)
    k = jnp.dot(m, wk_ref[...], preferred_element_type=F32)
    v = jnp.dot(m, wv_ref[...], preferred_element_type=F32)
    mk_ref[...] = k
    mv_ref[...] = v
    mkt_ref[...] = k.T.astype(BF16)
    mvb_ref[...] = v.astype(BF16)


def _mem_kv(mem, wk, wv):
    B, N, D = mem.shape
    blk = lambda b: (b, 0, 0)
    return pl.pallas_call(
        _mem_kv_kernel,
        grid=(B,),
        in_specs=[pl.BlockSpec((None, N, D), blk), _const_spec((D, D)), _const_spec((D, D))],
        out_specs=(pl.BlockSpec((None, N, D), blk), pl.BlockSpec((None, N, D), blk),
                   pl.BlockSpec((None, D, N), blk), pl.BlockSpec((None, N, D), blk)),
        out_shape=(jax.ShapeDtypeStruct((B, N, D), F32), jax.ShapeDtypeStruct((B, N, D), F32),
                   jax.ShapeDtypeStruct((B, D, N), BF16), jax.ShapeDtypeStruct((B, N, D), BF16)),
        compiler_params=_params(("arbitrary",)),
        name="mem_kv",
    )(mem, wk, wv)


def _ffn_kernel(*refs, seq_mode, final_norm, tm):
    if seq_mode:
        (x1_ref, oc_ref, wo_ref, g_ref, wup_ref, cw_ref, cb_ref, wdn_ref, gfin_ref,
         out_ref, up_out_ref, x2_ref, h_ref, act_ref, halo_ref) = refs
    else:
        (x1_ref, oc_ref, p2_ref, p1_ref, wo_ref, g_ref, wup_ref, cw_ref, cb_ref, wdn_ref, gfin_ref,
         out_ref, up_out_ref, x2_ref, h_ref, act_ref) = refs
    t = pl.program_id(1)

    if seq_mode:
        @pl.when(t == 0)
        def _():
            halo_ref[...] = jnp.zeros(halo_ref.shape, F32)

    x2 = x1_ref[...] + jnp.dot(oc_ref[...], wo_ref[...], preferred_element_type=F32)
    x2_ref[...] = x2
    h_ref[...] = _rms(x2, g_ref[...], EPS).astype(BF16)

    def up_proj(c0):
        cols_a = slice(c0, c0 + FF_CHUNK)
        cols_g = slice(D_FF + c0, D_FF + c0 + FF_CHUNK)
        return (jnp.dot(h_ref[...], wup_ref[:, cols_a], preferred_element_type=F32),
                jnp.dot(h_ref[...], wup_ref[:, cols_g], preferred_element_type=F32))

    def conv_part(up, col0):
        cols = slice(col0, col0 + FF_CHUNK)
        if seq_mode:
            ext = jnp.concatenate([halo_ref[:, cols], up], axis=0)
            um1 = ext[SUBLANES - 1:SUBLANES - 1 + tm]
            um2 = ext[SUBLANES - 2:SUBLANES - 2 + tm]
            halo_ref[:, cols] = up[tm - SUBLANES:, :]
            up_out_ref[:, cols] = up[tm - SUBLANES:, :]
        else:
            um1 = p1_ref[:, cols]
            um2 = p2_ref[:, cols]
            up_out_ref[:, cols] = up
        c = cb_ref[:, cols] + um2 * cw_ref[0:1, cols]
        c = c + um1 * cw_ref[1:2, cols]
        return c + up * cw_ref[2:3, cols]

    starts = list(range(0, D_FF, FF_CHUNK))
    nxt = up_proj(starts[0])
    for n, c0 in enumerate(starts):
        up_a, up_g = nxt
        if n + 1 < len(starts):
            nxt = up_proj(starts[n + 1])
        a = conv_part(up_a, c0)
        g = conv_part(up_g, D_FF + c0)
        act_ref[:, c0:c0 + FF_CHUNK] = (g * (1.0 / (1.0 + jnp.exp(-g))) * a).astype(BF16)

    x3 = x2_ref[...] + jnp.dot(act_ref[...], wdn_ref[...], preferred_element_type=F32)
    if final_norm:
        x3 = _rms(x3, gfin_ref[...], EPS)
    out_ref[...] = x3


def _ffn(x1, oc, prev, wo, g_ffn, w_up, conv_w, conv_b, w_down, g_final, tm, final_norm):
    B, T, D = x1.shape
    seq_mode = prev is None
    W2 = 2 * D_FF
    row = lambda b, t: (b, t, 0)
    in_specs = [pl.BlockSpec((None, tm, D), row), pl.BlockSpec((None, tm, D), row)]
    args = [x1, oc]
    if not seq_mode:
        in_specs += [pl.BlockSpec((None, tm, W2), row), pl.BlockSpec((None, tm, W2), row)]
        args += list(prev)
    in_specs += [_const_spec((D, D)), _const_spec((1, D)), _const_spec((D, W2)),
                 _const_spec((CONV_W, W2)), _const_spec((1, W2)), _const_spec((D_FF, D)),
                 _const_spec((1, D))]
    args += [wo, g_ffn, w_up, conv_w, conv_b, w_down, g_final]
    scratch = [pltpu.VMEM((tm, D), F32), pltpu.VMEM((tm, D), BF16), pltpu.VMEM((tm, D_FF), BF16)]
    if seq_mode:
        up_rows = SUBLANES
        up_spec = pl.BlockSpec((None, SUBLANES, W2), lambda b, t: (b, 0, 0))
        scratch += [pltpu.VMEM((SUBLANES, W2), F32)]
    else:
        up_rows = T
        up_spec = pl.BlockSpec((None, tm, W2), row)
    return pl.pallas_call(
        functools.partial(_ffn_kernel, seq_mode=seq_mode, final_norm=final_norm, tm=tm),
        grid=(B, T // tm),
        in_specs=in_specs,
        out_specs=(pl.BlockSpec((None, tm, D), row), up_spec),
        out_shape=(jax.ShapeDtypeStruct((B, T, D), F32), jax.ShapeDtypeStruct((B, up_rows, W2), F32)),
        scratch_shapes=scratch,
        compiler_params=_params(("arbitrary", "arbitrary")),
        name="wo_ffn_seq" if seq_mode else "wo_ffn_step",
    )(*args)


def _in_proj_sample_kernel(x_ref, g_ref, w_ref, st_ref, pw_ref, ps_ref, z_ref, po_ref, nst_ref):
    h = _rms(x_ref[...], g_ref[...], EPS).astype(BF16)
    z = jnp.dot(h, w_ref[...], preferred_element_type=F32)
    z_ref[...] = z
    u = z[:, :POOL_WIDTH]

    def shifted(j, cols):
        return st_ref[POOL_PAST - j, :, cols]

    _pool_groups(u, shifted, lambda w: float(w), pw_ref, ps_ref, po_ref)
    for r in range(POOL_PAST - 1):
        nst_ref[r] = st_ref[r + 1]
    nst_ref[POOL_PAST - 1] = u


def _in_proj_sample(x, g, w_in, st_t, pool_w, pool_scale):
    M, D = x.shape
    return pl.pallas_call(
        _in_proj_sample_kernel,
        grid=(1,),
        in_specs=[_const_spec((M, D)), _const_spec((1, D)), _const_spec((D, IN_WIDTH)),
                  _const_spec((POOL_PAST, M, POOL_WIDTH)),
                  _const_spec((len(POOL_WINDOWS), POOL_GROUP, POOL_GROUP)),
                  _const_spec((1, POOL_WIDTH))],
        out_specs=(_const_spec((M, IN_WIDTH)), _const_spec((M, POOL_WIDTH)),
                   _const_spec((POOL_PAST, M, POOL_WIDTH))),
        out_shape=(jax.ShapeDtypeStruct((M, IN_WIDTH), F32),
                   jax.ShapeDtypeStruct((M, POOL_WIDTH), BF16),
                   jax.ShapeDtypeStruct((POOL_PAST, M, POOL_WIDTH), F32)),
        compiler_params=_params(("arbitrary",)),
        name="in_proj_sample",
    )(x, g, w_in, st_t, pool_w, pool_scale)


def _paged_attn_kernel(pt_ref, lamp_ref, q_ref, kn_ref, vn_ref, bias_ref, *rest, n_pages, lam0):
    del pt_ref
    k_refs = rest[:n_pages]
    v_refs = rest[n_pages:2 * n_pages]
    o_ref = rest[2 * n_pages]
    s_ref = rest[2 * n_pages + 1]
    rows = PAGE * N_HEADS
    nt_dims = (((1,), (1,)), ((), ()))

    q4 = q_ref[...] * (DIFF_HD ** -0.5)
    lane = lax.broadcasted_iota(jnp.int32, (N_HEADS, HEAD_W), 1)
    q_all = jnp.concatenate([jnp.where(lane < DIFF_HD, q4, 0.0),
                             jnp.where(lane >= DIFF_HD, q4, 0.0)], axis=0)
    q_all_b = q_all.astype(BF16)
    for p in range(n_pages):
        kp = k_refs[p][...].astype(BF16)
        s_ref[:, p * rows:(p + 1) * rows] = lax.dot_general(q_all_b, kp, nt_dims, preferred_element_type=F32)

    s = s_ref[...] + bias_ref[...]
    kn8 = jnp.concatenate([kn_ref[...], kn_ref[...]], axis=0)
    s_new = jnp.sum(q_all * kn8, axis=1, keepdims=True)
    m = jnp.maximum(jnp.max(s, axis=1, keepdims=True), s_new)
    pr = jnp.exp(s - m)
    pr_new = jnp.exp(s_new - m)
    l = jnp.sum(pr, axis=1, keepdims=True) + pr_new
    lam = _lambda_value(lamp_ref, lam0)
    rowi = lax.broadcasted_iota(jnp.int32, (2 * N_HEADS, 1), 0)
    coef = jnp.where(rowi < N_HEADS, 1.0, -lam) / l
    w = (pr * coef).astype(BF16)

    vn8 = jnp.concatenate([vn_ref[...], vn_ref[...]], axis=0)
    acc = (pr_new * coef) * vn8
    for p in range(n_pages):
        vp = v_refs[p][...].astype(BF16)
        acc = acc + jnp.dot(w[:, p * rows:(p + 1) * rows], vp, preferred_element_type=F32)
    o_ref[...] = acc[:N_HEADS] + acc[N_HEADS:]


def _paged_attn(zs, page_table, cache_k, cache_v, layer, lamp, lam0):
    Bs = zs.shape[0]
    n_pages = page_table.shape[1]
    n_past = n_pages * PAGE
    depth, n_phys = cache_k.shape[0], cache_k.shape[1]
    rows = PAGE * N_HEADS
    ck = cache_k.reshape(depth * n_phys, rows, HEAD_W)
    cv = cache_v.reshape(depth * n_phys, rows, HEAD_W)
    heads3 = lambda lo: zs[:, lo:lo + QK_WIDTH].reshape(Bs, N_HEADS, HEAD_W)
    q3 = heads3(POOL_WIDTH)
    kn3 = heads3(POOL_WIDTH + QK_WIDTH)
    vn3 = heads3(POOL_WIDTH + 2 * QK_WIDTH)

    slopes = 2.0 ** (-8.0 * np.arange(1, N_HEADS + 1) / N_HEADS)
    key_row = np.arange(n_past * N_HEADS)
    key_pos, key_head = key_row // N_HEADS, key_row % N_HEADS
    row_head = np.arange(2 * N_HEADS) % N_HEADS
    alibi = -slopes[row_head][:, None] * (n_past - key_pos)[None, :].astype(np.float64)
    bias = jnp.asarray(np.where(row_head[:, None] == key_head[None, :], alibi, -np.inf), F32)

    def page_spec(p):
        return pl.BlockSpec((None, rows, HEAD_W), lambda b, pt: (layer * n_phys + pt[b, p], 0, 0))

    const2 = lambda b, pt: (0, 0)
    row3 = pl.BlockSpec((None, N_HEADS, HEAD_W), lambda b, pt: (b, 0, 0))
    grid_spec = pltpu.PrefetchScalarGridSpec(
        num_scalar_prefetch=1,
        grid=(Bs,),
        in_specs=[pl.BlockSpec((4, DIFF_HD), const2), row3, row3, row3,
                  pl.BlockSpec((2 * N_HEADS, n_past * N_HEADS), const2)]
                 + [page_spec(p) for p in range(n_pages)]
                 + [page_spec(p) for p in range(n_pages)],
        out_specs=row3,
        scratch_shapes=[pltpu.VMEM((2 * N_HEADS, n_past * N_HEADS), F32)],
    )
    return pl.pallas_call(
        functools.partial(_paged_attn_kernel, n_pages=n_pages, lam0=lam0),
        grid_spec=grid_spec,
        out_shape=jax.ShapeDtypeStruct((Bs, N_HEADS, HEAD_W), F32),
        compiler_params=_params(("arbitrary",)),
        name="paged_diff_attn",
    )(page_table, lamp, q3, kn3, vn3, bias, *([ck] * n_pages), *([cv] * n_pages))


def _cross_sample_kernel(q_ref, k_ref, v_ref, o_ref):
    n_rows = N_MEM * N_CROSS_HEADS * 2
    groups = 2 * N_CROSS_HEADS
    nt_dims = (((1,), (1,)), ((), ()))
    row = lax.broadcasted_iota(jnp.int32, (groups, n_rows), 0)
    lane = lax.broadcasted_iota(jnp.int32, (groups, n_rows), 1)
    own_lane = (lane & (groups - 1)) == row
    head_lane = (lane & (groups - 1)) == (row & (N_CROSS_HEADS - 1))
    n_b = q_ref.shape[0]
    raw = [lax.dot_general(q_ref[i], k_ref[i].astype(BF16), nt_dims, preferred_element_type=F32)
           for i in range(n_b)]
    weights = []
    for s8 in raw:
        part = jnp.sum(jnp.where(own_lane, s8, 0.0), axis=0, keepdims=True)
        full = part + pltpu.roll(part, n_rows - N_CROSS_HEADS, axis=1)
        s = jnp.where(head_lane, jnp.broadcast_to(full, (groups, n_rows)), NEG_INF)
        m = jnp.max(s, axis=1, keepdims=True)
        p = jnp.exp(s - m)
        a = p / jnp.sum(p, axis=1, keepdims=True)
        w = jnp.where(row < N_CROSS_HEADS, a, pltpu.roll(a, N_CROSS_HEADS, axis=1))
        weights.append(w.astype(BF16))
    for i, w in enumerate(weights):
        o_ref[i] = jnp.dot(w, v_ref[i].astype(BF16),
                           preferred_element_type=F32).astype(o_ref.dtype)


def _cross_sample(qc, mem_k, mem_v, layer):
    Bs = qc.shape[0]
    depth = mem_k.shape[0]
    half = CROSS_HD // 2
    n_rows = N_MEM * N_CROSS_HEADS * 2

    def stored_rows(mem):
        m6 = mem.reshape(depth, Bs, N_MEM, N_CROSS_HEADS, 2, half)
        return jnp.transpose(m6, (0, 1, 2, 4, 3, 5)).reshape(depth * Bs, n_rows, half)

    q8 = jnp.transpose(qc.reshape(Bs, N_CROSS_HEADS, 2, half), (0, 2, 1, 3)).reshape(Bs, 2 * N_CROSS_HEADS, half)
    nb = CROSS_ROWS_PER_STEP
    mem_blk = pl.BlockSpec((nb, n_rows, half), lambda b: (layer * (Bs // nb) + b, 0, 0))
    blk = lambda b: (b, 0, 0)
    o8 = pl.pallas_call(
        _cross_sample_kernel,
        grid=(Bs // nb,),
        in_specs=[pl.BlockSpec((nb, 2 * N_CROSS_HEADS, half), blk), mem_blk, mem_blk],
        out_specs=pl.BlockSpec((nb, 2 * N_CROSS_HEADS, half), blk),
        out_shape=jax.ShapeDtypeStruct((Bs, 2 * N_CROSS_HEADS, half), BF16),
        compiler_params=_params(("arbitrary",)),
        name="cross_sample",
    )(q8, stored_rows(mem_k), stored_rows(mem_v))
    return jnp.transpose(o8.reshape(Bs, 2, N_CROSS_HEADS, half), (0, 2, 1, 3)).reshape(Bs, D_MODEL)


def kernel(x_prompt, x_sample, mem_prompt, cache_k, cache_v, cache_mem_k, cache_mem_v, state_pool, state_conv, page_table, g_mix, w_in, pool_w, pool_scale, lam_q1, lam_k1, lam_q2, lam_k2, subln_g, w_out, g_cross, wq_c, wk_c, wv_c, wo_c, g_ffn, w_up, conv_w, conv_b, w_down, g_final):
    depth = w_in.shape[0]
    Bp, T, D = x_prompt.shape
    Bs = x_sample.shape[0]
    W2 = 2 * D_FF
    bf = lambda a: a.astype(BF16)
    w_in_b, pool_w_b, w_out_b = bf(w_in), bf(pool_w), bf(w_out)
    wq_b, wk_b, wv_b, wo_b, w_up_b, w_down_b = bf(wq_c), bf(wk_c), bf(wv_c), bf(wo_c), bf(w_up), bf(w_down)
    gfin = g_final.reshape(1, D)

    xp = x_prompt
    xs = x_sample.reshape(Bs, D)
    outs = {n: [] for n in ("kp", "vp", "mkp", "mvp", "pp", "cp", "ks", "vs", "ps", "cs")}
    for l in range(depth):
        lam0 = _lambda_init(l)
        last = l == depth - 1
        g_mix_l = g_mix[l].reshape(1, D)
        g_cross_l = g_cross[l].reshape(1, D)
        g_ffn_l = g_ffn[l].reshape(1, D)
        ps_l = pool_scale[l].reshape(1, POOL_WIDTH)
        cb_l = conv_b[l].reshape(1, W2)
        lamp = jnp.stack([lam_q1[l], lam_k1[l], lam_q2[l], lam_k2[l]])

        mk, mv, mkt, mvb = _mem_kv(mem_prompt, wk_b[l], wv_b[l])
        kp, vp, qt, kb, vt, po, tail = _in_proj_prompt(xp, g_mix_l, w_in_b[l], pool_w_b[l], ps_l)
        o = _flash_diff_attn(qt, kb, vt, lamp, subln_g[l].reshape(HEAD_W, 1), lam0)
        x1, oc = _out_q(xp, po, o, subln_g[l].reshape(1, HEAD_W), w_out_b[l], g_cross_l, wq_b[l], PROJ_TILE,
                        memory=(mkt, mvb))
        xp, up_tail = _ffn(x1, oc, None, wo_b[l], g_ffn_l, w_up_b[l], conv_w[l], cb_l,
                           w_down_b[l], gfin, ROW_TILE, last)
        outs["kp"].append(kp.reshape(Bp, T, N_HEADS, HEAD_W))
        outs["vp"].append(vp.reshape(Bp, T, N_HEADS, HEAD_W))
        outs["mkp"].append(mk.reshape(Bp, N_MEM, N_CROSS_HEADS, CROSS_HD))
        outs["mvp"].append(mv.reshape(Bp, N_MEM, N_CROSS_HEADS, CROSS_HD))
        outs["pp"].append(tail[:, 2 * SUBLANES - POOL_PAST:])
        outs["cp"].append(up_tail[:, SUBLANES - (CONV_W - 1):])

        st_t = jnp.swapaxes(state_pool[l], 0, 1)
        zs, pos, nst = _in_proj_sample(xs, g_mix_l, w_in_b[l], st_t, pool_w_b[l], ps_l)
        o_raw = _paged_attn(zs, page_table, cache_k, cache_v, l, lamp, lam0)
        x1s, qcs = _out_q(xs.reshape(1, Bs, D), pos.reshape(1, Bs, -1), o_raw.reshape(1, Bs, -1),
                          subln_g[l].reshape(1, HEAD_W), w_out_b[l], g_cross_l, wq_b[l], Bs,
                          subln_scale=1.0 - lam0)
        ocs = _cross_sample(qcs.reshape(Bs, D), cache_mem_k, cache_mem_v, l)
        prev = (state_conv[l][:, 0].reshape(1, Bs, W2), state_conv[l][:, 1].reshape(1, Bs, W2))
        xs3, up_s = _ffn(x1s, ocs.reshape(1, Bs, D), prev, wo_b[l], g_ffn_l, w_up_b[l],
                         conv_w[l], cb_l, w_down_b[l], gfin, Bs, last)
        xs = xs3.reshape(Bs, D)
        outs["ks"].append(zs[:, POOL_WIDTH + QK_WIDTH:POOL_WIDTH + 2 * QK_WIDTH].reshape(Bs, 1, N_HEADS, HEAD_W))
        outs["vs"].append(zs[:, POOL_WIDTH + 2 * QK_WIDTH:].reshape(Bs, 1, N_HEADS, HEAD_W))
        outs["ps"].append(jnp.swapaxes(nst, 0, 1))
        outs["cs"].append(jnp.stack([state_conv[l][:, 1], up_s.reshape(Bs, W2)], axis=1))

    st = lambda n: jnp.stack(outs[n])
    return (xp, xs.reshape(Bs, 1, D), st("kp"), st("vp"), st("mkp"), st("mvp"), st("pp"), st("cp"),
            st("ks"), st("vs"), st("ps"), st("cs"))
```
